```python
import math
import jax, jax.numpy as jnp
from jax import lax
import numpy as np

D_MODEL = 1024
BATCH = 4
SEQ = 4096
DEPTH = 4
DEC_BATCH = 32
DEC_SEQ = 8
PAST_LEN = 8192
PAGE_SIZE = 128

HEAD_DIM = 64
N_POOL_GROUPS = 4
POOL_GROUP_DIM = D_MODEL // 16
POOL_WINDOWS = (2, 4, 8, 16)
POOL_BUF = max(POOL_WINDOWS) - 1
W_POOL = N_POOL_GROUPS * POOL_GROUP_DIM
H_DIFF = D_MODEL // 256
W_DIFF = H_DIFF * 2 * HEAD_DIM
H_FOX = D_MODEL // 256
W_FOX = H_FOX * HEAD_DIM
W_CONV = D_MODEL // 4
CONV_WIDTH = 3
CONV_BUF = CONV_WIDTH - 1
N_BRANCH = 4
D_FF = ((8 * D_MODEL // 3 + 127) // 128) * 128
ROPE_THETA = 10000.0
Q_BLOCK = 128
LN_EPS = 1e-5
RMS_EPS = 1e-5
DEEPNORM_ALPHA = (2.0 * DEPTH) ** 0.25
DEEPNORM_BETA = (8.0 * DEPTH) ** -0.25
NEG_INF = -1e30

IN_SIZES = (W_POOL, W_DIFF, W_DIFF, W_DIFF, W_FOX, W_FOX, W_FOX, H_FOX,
            W_CONV, W_CONV, W_CONV, N_BRANCH * D_MODEL)
IN_COLS = sum(IN_SIZES)
IN_OFFSETS = [int(o) for o in np.cumsum(IN_SIZES)[:-1]]
BRANCH_WIDTHS = (W_POOL, W_DIFF, W_FOX, W_CONV)
W_MIX = sum(BRANCH_WIDTHS)
BRANCH_OFFSETS = [int(o) for o in np.cumsum(BRANCH_WIDTHS)[:-1]]

kernel_name = "hybrid_gated_branch_decoder_step"


def _layer_norm(x, g, b):
    xf = x.astype(jnp.float32)
    mu = jnp.mean(xf, axis=-1, keepdims=True)
    var = jnp.mean(jnp.square(xf - mu), axis=-1, keepdims=True)
    y = (xf - mu) * lax.rsqrt(var + LN_EPS) * g.astype(jnp.float32) + b.astype(jnp.float32)
    return y.astype(x.dtype)


def _swiglu(x, wg, wu, wd):
    return (jax.nn.silu(x @ wg) * (x @ wu)) @ wd


def _rope(x, pos):
    half = x.shape[-1] // 2
    inv = ROPE_THETA ** (-jnp.arange(half, dtype=jnp.float32) / half)
    ang = pos.astype(jnp.float32)[:, None] * inv[None, :]
    cos = jnp.cos(ang)[None, :, None, :]
    sin = jnp.sin(ang)[None, :, None, :]
    x1 = x[..., :half].astype(jnp.float32)
    x2 = x[..., half:].astype(jnp.float32)
    return jnp.concatenate([x1 * cos - x2 * sin, x2 * cos + x1 * sin], axis=-1).astype(x.dtype)


def _causal_probs(logits, q_pos, k_pos):
    mask = k_pos[None, :] <= q_pos[:, None]
    return jax.nn.softmax(jnp.where(mask, logits, NEG_INF), axis=-1)


def _diff_block(q, q_pos, k, v, k_pos, lam):
    s = jnp.einsum('bqhid,bkhid->bihqk', q, k, preferred_element_type=jnp.float32) / math.sqrt(HEAD_DIM)
    p = _causal_probs(s, q_pos, k_pos)
    w = p[:, 0] - lam * p[:, 1]
    return jnp.einsum('bhqk,bkhe->bqhe', w.astype(v.dtype), v)


def _fox_block(q, fq, q_pos, k, fk, v, k_pos):
    s = jnp.einsum('bqhd,bkhd->bhqk', q, k, preferred_element_type=jnp.float32) / math.sqrt(HEAD_DIM)
    s = s + (jnp.transpose(fq, (0, 2, 1))[..., :, None] - jnp.transpose(fk, (0, 2, 1))[..., None, :])
    p = _causal_probs(s, q_pos, k_pos)
    return jnp.einsum('bhqk,bkhd->bqhd', p.astype(v.dtype), v)


def _sweep_query_blocks(fn, q_args, q_pos):
    t = q_pos.shape[0]
    nb = t // Q_BLOCK
    xs = tuple(jnp.moveaxis(a.reshape(a.shape[0], nb, Q_BLOCK, *a.shape[2:]), 1, 0) for a in q_args)
    xs = xs + (q_pos.reshape(nb, Q_BLOCK),)
    out = lax.map(lambda blk: fn(*blk), xs)
    out = jnp.moveaxis(out, 0, 1)
    return out.reshape(out.shape[0], t, *out.shape[3:])


def _pool_mixer(u, prefix, pos, pool_w, pool_scale):
    b, t, _ = u.shape
    uu = jnp.concatenate([prefix.astype(u.dtype), u], axis=1)
    cs = jnp.cumsum(uu.astype(jnp.float32), axis=1)
    z = jnp.concatenate([jnp.zeros((b, 1, W_POOL), jnp.float32), cs], axis=1)
    hi = z[:, POOL_BUF + 1:]
    means = []
    for g, w in enumerate(POOL_WINDOWS):
        sl = slice(g * POOL_GROUP_DIM, (g + 1) * POOL_GROUP_DIM)
        lo = z[:, POOL_BUF + 1 - w: POOL_BUF + 1 - w + t, sl]
        cnt = jnp.minimum(pos + 1, w).astype(jnp.float32)[None, :, None]
        means.append((hi[..., sl] - lo) / cnt)
    pooled = (jnp.concatenate(means, axis=-1) - u.astype(jnp.float32)).astype(u.dtype)
    pooled = pooled.reshape(b, t, N_POOL_GROUPS, POOL_GROUP_DIM)
    y = jnp.einsum('btgc,gcd->btgd', pooled, pool_w).reshape(b, t, W_POOL) * pool_scale
    return y, uu[:, -POOL_BUF:]


def _conv_mixer(gb, gc, h, prefix, conv_w):
    v = gc * h
    vv = jnp.concatenate([prefix.astype(v.dtype), v], axis=1)
    t = v.shape[1]
    out = conv_w[0] * vv[:, 0:t]
    for k in range(1, CONV_WIDTH):
        out = out + conv_w[k] * vv[:, k:k + t]
    return gb * out, vv[:, -CONV_BUF:]


def _mixing(x, pos, k_pos, layer, w_in, b_fgate, pool_w, pool_scale, lambda_qk, subln_g,
            conv_w, w_branch, w_o, past):
    b, t, _ = x.shape
    u = x @ w_in
    (u_pool, q_d, k_d, v_d, q_f, k_f, v_f, f_f, c_b, c_c, c_h, u_g) = jnp.split(u, IN_OFFSETS, axis=-1)

    pool_prefix = jnp.zeros((b, POOL_BUF, W_POOL), x.dtype) if past is None else past[5]
    y_a, new_pool = _pool_mixer(u_pool, pool_prefix, pos, pool_w, pool_scale)

    q_d = _rope(q_d.reshape(b, t, H_DIFF * 2, HEAD_DIM), pos).reshape(b, t, H_DIFF, 2, HEAD_DIM)
    k_d = _rope(k_d.reshape(b, t, H_DIFF * 2, HEAD_DIM), pos).reshape(b, t, H_DIFF, 2 * HEAD_DIM)
    v_d = v_d.reshape(b, t, H_DIFF, 2 * HEAD_DIM)
    lam_init = 0.8 - 0.6 * math.exp(-0.3 * layer)
    lq = lambda_qk.astype(jnp.float32)
    lam = jnp.exp(jnp.sum(lq[0] * lq[1])) - jnp.exp(jnp.sum(lq[2] * lq[3])) + lam_init
    if past is None:
        kd_all, vd_all = k_d, v_d
    else:
        kd_all = jnp.concatenate([past[0].astype(x.dtype), k_d], axis=1)
        vd_all = jnp.concatenate([past[1].astype(x.dtype), v_d], axis=1)
    kd5 = kd_all.reshape(b, kd_all.shape[1], H_DIFF, 2, HEAD_DIM)
    diff_fn = lambda qb, pb: _diff_block(qb, pb, kd5, vd_all, k_pos, lam)
    o_d = _sweep_query_blocks(diff_fn, (q_d,), pos) if past is None else diff_fn(q_d, pos)
    of = o_d.astype(jnp.float32)
    of = of * lax.rsqrt(jnp.mean(jnp.square(of), axis=-1, keepdims=True) + RMS_EPS)
    y_b = (of * subln_g.astype(jnp.float32) * (1.0 - lam_init)).astype(x.dtype).reshape(b, t, W_DIFF)

    q_f = q_f.reshape(b, t, H_FOX, HEAD_DIM)
    k_f = k_f.reshape(b, t, H_FOX, HEAD_DIM)
    v_f = v_f.reshape(b, t, H_FOX, HEAD_DIM)
    logf = jax.nn.log_sigmoid((f_f + b_fgate).astype(jnp.float32))
    if past is None:
        kf_all, vf_all, logf_all = k_f, v_f, logf
    else:
        kf_all = jnp.concatenate([past[2].astype(x.dtype), k_f], axis=1)
        vf_all = jnp.concatenate([past[3].astype(x.dtype), v_f], axis=1)
        logf_all = jnp.concatenate([past[4].astype(jnp.float32), logf], axis=1)
    f_cum = jnp.cumsum(logf_all, axis=1)
    f_q = f_cum[:, -t:]
    fox_fn = lambda qb, fqb, pb: _fox_block(qb, fqb, pb, kf_all, f_cum, vf_all, k_pos)
    o_f = _sweep_query_blocks(fox_fn, (q_f, f_q), pos) if past is None else fox_fn(q_f, f_q, pos)
    y_c = o_f.reshape(b, t, W_FOX)

    conv_prefix = jnp.zeros((b, CONV_BUF, W_CONV), x.dtype) if past is None else past[6]
    y_dc, new_conv = _conv_mixer(c_b, c_c, c_h, conv_prefix, conv_w)

    ys = (y_a, y_b, y_c, y_dc)
    p_blocks = jnp.split(w_branch, BRANCH_OFFSETS, axis=0)
    gates = jax.nn.sigmoid(u_g.astype(jnp.float32)).astype(x.dtype).reshape(b, t, N_BRANCH, D_MODEL)
    merged = gates[:, :, 0] * (ys[0] @ p_blocks[0])
    for i in range(1, N_BRANCH):
        merged = merged + gates[:, :, i] * (ys[i] @ p_blocks[i])
    out = merged @ w_o
    return out, (k_d, v_d, k_f, v_f, logf, new_pool, new_conv)


def _trunk(x, pos, k_pos, past_fn, w_in, b_fgate, pool_w, pool_scale, lambda_qk, subln_g, conv_w,
           w_branch, w_o, w_ffn1_gate, w_ffn1_up, w_ffn1_down, w_ffn2_gate, w_ffn2_up, w_ffn2_down,
           ln_g, ln_b):
    rows = []
    for l in range(DEPTH):
        x = _layer_norm(DEEPNORM_ALPHA * x + 0.5 * _swiglu(x, w_ffn1_gate[l], w_ffn1_up[l], w_ffn1_down[l]),
                        ln_g[l, 0], ln_b[l, 0])
        m, r = _mixing(x, pos, k_pos, l, w_in[l], b_fgate[l], pool_w[l], pool_scale[l], lambda_qk[l],
                       subln_g[l], conv_w[l], w_branch[l], w_o[l], past_fn(l))
        x = _layer_norm(DEEPNORM_ALPHA * x + m, ln_g[l, 1], ln_b[l, 1])
        x = _layer_norm(DEEPNORM_ALPHA * x + 0.5 * _swiglu(x, w_ffn2_gate[l], w_ffn2_up[l], w_ffn2_down[l]),
                        ln_g[l, 2], ln_b[l, 2])
        rows.append(r)
    stacked = [jnp.stack([r[i] for r in rows], axis=0) for i in range(7)]
    return x, stacked


def setup_inputs(seed: int = 0) -> dict:
    key = jax.random.key(seed)
    ks = jax.random.split(key, 40)
    n_pages = PAST_LEN // PAGE_SIZE
    n_pool_pages = (5 * DEC_BATCH * n_pages) // 4
    nrm = lambda k, shape: jax.random.normal(k, shape, jnp.float32)

    x_prompt = nrm(ks[0], (BATCH, SEQ, D_MODEL))
    x_sample = nrm(ks[1], (DEC_BATCH, DEC_SEQ, D_MODEL))
    cache_diff_k = nrm(ks[2], (DEPTH, n_pool_pages, PAGE_SIZE, H_DIFF, 2 * HEAD_DIM))
    cache_diff_v = nrm(ks[3], (DEPTH, n_pool_pages, PAGE_SIZE, H_DIFF, 2 * HEAD_DIM))
    cache_fox_k = nrm(ks[4], (DEPTH, n_pool_pages, PAGE_SIZE, H_FOX, HEAD_DIM))
    cache_fox_v = nrm(ks[5], (DEPTH, n_pool_pages, PAGE_SIZE, H_FOX, HEAD_DIM))
    cache_fox_logf = jax.nn.log_sigmoid(3.0 + nrm(ks[6], (DEPTH, n_pool_pages, PAGE_SIZE, H_FOX)))
    state_pool = nrm(ks[7], (DEPTH, DEC_BATCH, POOL_BUF, W_POOL))
    state_conv = nrm(ks[8], (DEPTH, DEC_BATCH, CONV_BUF, W_CONV))
    page_table = jax.random.permutation(ks[9], n_pool_pages)[:DEC_BATCH * n_pages]
    page_table = page_table.reshape(DEC_BATCH, n_pages).astype(jnp.int32)

    w_in = nrm(ks[10], (DEPTH, D_MODEL, IN_COLS)) * D_MODEL ** -0.5
    b_fgate = 3.0 + 0.5 * nrm(ks[11], (DEPTH, H_FOX))
    pool_w = nrm(ks[12], (DEPTH, N_POOL_GROUPS, POOL_GROUP_DIM, POOL_GROUP_DIM)) * POOL_GROUP_DIM ** -0.5
    pool_scale = 1.0 + 0.02 * nrm(ks[13], (DEPTH, W_POOL))
    lambda_qk = 0.1 * nrm(ks[14], (DEPTH, 4, HEAD_DIM))
    subln_g = 1.0 + 0.02 * nrm(ks[15], (DEPTH, 2 * HEAD_DIM))
    conv_w = nrm(ks[16], (DEPTH, CONV_WIDTH, W_CONV)) * CONV_WIDTH ** -0.5
    w_branch = jnp.concatenate(
        [nrm(ks[17 + i], (DEPTH, wb, D_MODEL)) * wb ** -0.5 for i, wb in enumerate(BRANCH_WIDTHS)], axis=1)
    w_o = nrm(ks[21], (DEPTH, D_MODEL, D_MODEL)) * D_MODEL ** -0.5 * DEEPNORM_BETA
    w_ffn1_gate = nrm(ks[22], (DEPTH, D_MODEL, D_FF)) * D_MODEL ** -0.5
    w_ffn1_up = nrm(ks[23], (DEPTH, D_MODEL, D_FF)) * D_MODEL ** -0.5
    w_ffn1_down = nrm(ks[24], (DEPTH, D_FF, D_MODEL)) * D_FF ** -0.5 * DEEPNORM_BETA
    w_ffn2_gate = nrm(ks[25], (DEPTH, D_MODEL, D_FF)) * D_MODEL ** -0.5
    w_ffn2_up = nrm(ks[26], (DEPTH, D_MODEL, D_FF)) * D_MODEL ** -0.5
    w_ffn2_down = nrm(ks[27], (DEPTH, D_FF, D_MODEL)) * D_FF ** -0.5 * DEEPNORM_BETA
    ln_g = 1.0 + 0.02 * nrm(ks[28], (DEPTH, 3, D_MODEL))
    ln_b = 0.02 * nrm(ks[29], (DEPTH, 3, D_MODEL))
    return {"x_prompt": x_prompt, "x_sample": x_sample,
            "cache_diff_k": cache_diff_k, "cache_diff_v": cache_diff_v,
            "cache_fox_k": cache_fox_k, "cache_fox_v": cache_fox_v, "cache_fox_logf": cache_fox_logf,
            "state_pool": state_pool, "state_conv": state_conv, "page_table": page_table,
            "w_in": w_in, "b_fgate": b_fgate, "pool_w": pool_w, "pool_scale": pool_scale,
            "lambda_qk": lambda_qk, "subln_g": subln_g, "conv_w": conv_w, "w_branch": w_branch,
            "w_o": w_o, "w_ffn1_gate": w_ffn1_gate, "w_ffn1_up": w_ffn1_up, "w_ffn1_down": w_ffn1_down,
            "w_ffn2_gate": w_ffn2_gate, "w_ffn2_up": w_ffn2_up, "w_ffn2_down": w_ffn2_down,
            "ln_g": ln_g, "ln_b": ln_b}


def reference(x_prompt, x_sample, cache_diff_k, cache_diff_v, cache_fox_k, cache_fox_v, cache_fox_logf,
              state_pool, state_conv, page_table, w_in, b_fgate, pool_w, pool_scale, lambda_qk, subln_g,
              conv_w, w_branch, w_o, w_ffn1_gate, w_ffn1_up, w_ffn1_down, w_ffn2_gate, w_ffn2_up,
              w_ffn2_down, ln_g, ln_b):
    n_pages = PAST_LEN // PAGE_SIZE
    weights = (w_in, b_fgate, pool_w, pool_scale, lambda_qk, subln_g, conv_w, w_branch, w_o,
               w_ffn1_gate, w_ffn1_up, w_ffn1_down, w_ffn2_gate, w_ffn2_up, w_ffn2_down, ln_g, ln_b)

    pos_p = jnp.arange(SEQ, dtype=jnp.int32)
    y_prompt, p_rows = _trunk(x_prompt, pos_p, pos_p, lambda l: None, *weights)
    p_diff_k, p_diff_v, p_fox_k, p_fox_v, p_fox_logf, p_pool, p_conv = p_rows

    def gather(pool):
        g = pool[page_table]
        return g.reshape(g.shape[0], n_pages * PAGE_SIZE, *pool.shape[2:])

    def past_fn(l):
        return (gather(cache_diff_k[l]), gather(cache_diff_v[l]), gather(cache_fox_k[l]),
                gather(cache_fox_v[l]), gather(cache_fox_logf[l]), state_pool[l], state_conv[l])

    pos_s = PAST_LEN + jnp.arange(DEC_SEQ, dtype=jnp.int32)
    kpos_s = jnp.arange(PAST_LEN + DEC_SEQ, dtype=jnp.int32)
    y_sample, s_rows = _trunk(x_sample, pos_s, kpos_s, past_fn, *weights)
    s_diff_k, s_diff_v, s_fox_k, s_fox_v, s_fox_logf, s_pool, s_conv = s_rows

    return (y_prompt, y_sample, p_diff_k, p_diff_v, p_fox_k, p_fox_v, p_fox_logf, p_pool, p_conv,
            s_diff_k, s_diff_v, s_fox_k, s_fox_v, s_fox_logf, s_pool, s_conv)
```

```python
import functools
import math

import jax
import jax.numpy as jnp
import numpy as np
from jax import lax
from jax.experimental import pallas as pl
from jax.experimental.pallas import tpu as pltpu

F32 = jnp.float32
BF16 = jnp.bfloat16

HEAD_DIM = 64
ROPE_THETA = 10000.0
LN_EPS = 1e-5
RMS_EPS = 1e-5
NEG_INF = -1e30
POOL_WINDOWS = (2, 4, 8, 16)
HALO_ROWS = 16
LANES = 128
VMEM_LIMIT_BYTES = 56 * 1024 * 1024


def _cparams(*sem):
    return pltpu.CompilerParams(dimension_semantics=sem, vmem_limit_bytes=VMEM_LIMIT_BYTES)


def _const_spec(shape):
    zeros = (0,) * len(shape)
    return pl.BlockSpec(shape, lambda *_: zeros, pipeline_mode=pl.Buffered(1))


def _dot(a, b):
    return jnp.dot(a, b, preferred_element_type=F32)


def _dot_nt(a, b):
    return lax.dot_general(a, b, (((1,), (1,)), ((), ())), preferred_element_type=F32)


def _rep(x, n):
    return x if n == 1 else jnp.concatenate([x] * n, axis=1)


def _layer_norm(z, g, b):
    mu = jnp.mean(z, axis=-1, keepdims=True)
    zc = z - mu
    var = jnp.mean(zc * zc, axis=-1, keepdims=True)
    return zc * lax.rsqrt(var + LN_EPS) * g + b


def _ffn_ln_body(x_ref, wg_ref, wu_ref, wd_ref, g_ref, b_ref, o_ref, acc_ref, *, chunk, alpha):
    x = x_ref[...]
    xb = x.astype(BF16)
    d_ff = wg_ref.shape[1]
    for c in range(d_ff // chunk):
        sl = slice(c * chunk, (c + 1) * chunk)
        gate = _dot(xb, wg_ref[:, sl])
        up = _dot(xb, wu_ref[:, sl])
        h = gate * jax.nn.sigmoid(gate) * up
        d = _dot(h.astype(BF16), wd_ref[sl, :])
        if c == 0:
            acc_ref[...] = d
        else:
            acc_ref[...] += d
    z = alpha * x + 0.5 * acc_ref[...]
    o_ref[...] = _layer_norm(z, g_ref[...], b_ref[...])


def _ffn_ln(x, wg, wu, wd, g, b, *, tm, alpha):
    m, d = x.shape
    d_ff = wg.shape[1]
    row = pl.BlockSpec((tm, d), lambda i: (i, 0))
    return pl.pallas_call(
        functools.partial(_ffn_ln_body, chunk=256, alpha=alpha),
        out_shape=jax.ShapeDtypeStruct((m, d), F32),
        grid=(m // tm,),
        in_specs=[row, _const_spec((d, d_ff)), _const_spec((d, d_ff)), _const_spec((d_ff, d)),
                  _const_spec((1, d)), _const_spec((1, d))],
        out_specs=row,
        scratch_shapes=[pltpu.VMEM((tm, d), F32)],
        compiler_params=_cparams("parallel"),
        name="ffn_ln",
    )(x, wg, wu, wd, g, b)


def _in_proj_body(x_ref, w_ref, wft_ref, bf_ref, cos_ref, sin_ref,
                  umix_ref, qd_ref, kd32_ref, kd16_ref, vd32_ref, vd16_ref,
                  qf_ref, kf32_ref, kf16_ref, vf32_ref, vf16_ref, logft_ref, *, d, q_scale):
    w_mix, w_diff, w_fox = d, d // 2, d // 4
    xb = x_ref[...].astype(BF16)
    tm = xb.shape[0]
    umix_ref[...] = _dot(xb, w_ref[:, 0:w_mix])

    qk = _dot(xb, w_ref[:, w_mix:w_mix + 2 * w_diff])
    cos = cos_ref[...]
    sin = sin_ref[...]
    lane = lax.broadcasted_iota(jnp.int32, (tm, LANES), 1)
    first_half = (lane & (HEAD_DIM // 2)) == 0
    n_blk = w_diff // LANES
    for j in range(2 * n_blk):
        blk = qk[:, j * LANES:(j + 1) * LANES]
        partner = jnp.where(first_half, pltpu.roll(blk, LANES - HEAD_DIM // 2, 1),
                            pltpu.roll(blk, HEAD_DIM // 2, 1))
        r = blk * cos + partner * sin
        if j < n_blk:
            qd_ref[:, j * LANES:(j + 1) * LANES] = (r * q_scale).astype(BF16)
        else:
            sl = slice((j - n_blk) * LANES, (j - n_blk + 1) * LANES)
            kd32_ref[:, sl] = r
            kd16_ref[:, sl] = r.astype(BF16)

    off = w_mix + 2 * w_diff
    vd = _dot(xb, w_ref[:, off:off + w_diff])
    vd32_ref[...] = vd
    vd16_ref[...] = vd.astype(BF16)

    off += w_diff
    fox = _dot(xb, w_ref[:, off:off + 3 * w_fox])
    qf_ref[...] = (fox[:, 0:w_fox] * q_scale).astype(BF16)
    kf = fox[:, w_fox:2 * w_fox]
    kf32_ref[...] = kf
    kf16_ref[...] = kf.astype(BF16)
    vf = fox[:, 2 * w_fox:3 * w_fox]
    vf32_ref[...] = vf
    vf16_ref[...] = vf.astype(BF16)

    ft = _dot_nt(wft_ref[...], xb)
    logft_ref[...] = jax.nn.log_sigmoid(ft[0:8, :] + bf_ref[...])


def _in_proj(x, w_main, w_ft, b_f, cos_t, sin_t, *, tm):
    m, d = x.shape
    n_tab = cos_t.shape[0] // tm
    w_diff, w_fox = d // 2, d // 4
    row = lambda w: pl.BlockSpec((tm, w), lambda i: (i, 0))
    tab = pl.BlockSpec((tm, LANES), lambda i: (i % n_tab, 0))
    sds = jax.ShapeDtypeStruct
    out_shape = [sds((m, d), F32),
                 sds((m, w_diff), BF16), sds((m, w_diff), F32), sds((m, w_diff), BF16),
                 sds((m, w_diff), F32), sds((m, w_diff), BF16),
                 sds((m, w_fox), BF16), sds((m, w_fox), F32), sds((m, w_fox), BF16),
                 sds((m, w_fox), F32), sds((m, w_fox), BF16),
                 sds((8, m), F32)]
    out_specs = [row(d), row(w_diff), row(w_diff), row(w_diff), row(w_diff), row(w_diff),
                 row(w_fox), row(w_fox), row(w_fox), row(w_fox), row(w_fox),
                 pl.BlockSpec((8, tm), lambda i: (0, i))]
    return pl.pallas_call(
        functools.partial(_in_proj_body, d=d, q_scale=1.0 / math.sqrt(HEAD_DIM)),
        out_shape=out_shape,
        grid=(m // tm,),
        in_specs=[row(d), _const_spec(w_main.shape), _const_spec(w_ft.shape), _const_spec(b_f.shape),
                  tab, tab],
        out_specs=out_specs,
        compiler_params=_cparams("parallel"),
        name="in_proj",
    )(x, w_main, w_ft, b_f, cos_t, sin_t)


def _mixers_body(u_ref, halo_ref, wbd_ref, pscale_ref, cw_ref, yad_ref, state_ref, ext_ref,
                 *, tm, pos0, w):
    out_dtype = yad_ref.dtype
    ti = pl.program_id(1)

    @pl.when(ti == 0)
    def _():
        ext_ref[0:HALO_ROWS, :] = halo_ref[0]

    u = u_ref[0]
    up = u[:, 0:w]
    cb = u[:, w:2 * w]
    ext_ref[HALO_ROWS:HALO_ROWS + tm, 0:w] = up
    ext_ref[HALO_ROWS:HALO_ROWS + tm, w:2 * w] = u[:, 2 * w:3 * w] * u[:, 3 * w:4 * w]
    e = ext_ref[...]
    ep = e[:, 0:w]
    ev = e[:, w:2 * w]

    sums = []
    b = ep
    for k in (1, 2, 4, 8):
        b = b + pltpu.roll(b, k, 0)
        sums.append(b[HALO_ROWS:, :])
    lane = lax.broadcasted_iota(jnp.int32, (tm, w), 1)
    grp = w // len(POOL_WINDOWS)
    win_sum = jnp.where(lane < grp, sums[0],
                        jnp.where(lane < 2 * grp, sums[1], jnp.where(lane < 3 * grp, sums[2], sums[3])))
    window = jnp.where(lane < grp, POOL_WINDOWS[0],
                       jnp.where(lane < 2 * grp, POOL_WINDOWS[1],
                                 jnp.where(lane < 3 * grp, POOL_WINDOWS[2], POOL_WINDOWS[3])))
    pos = pos0 + ti * tm + lax.broadcasted_iota(jnp.int32, (tm, w), 0)
    cnt = jnp.minimum(pos + 1, window).astype(F32)
    pooled = win_sum / cnt - up
    ya = _dot(pooled.astype(BF16), wbd_ref[...]) * pscale_ref[...]

    cw = cw_ref[...]
    conv = cw[0:1, :] * pltpu.roll(ev, 2, 0) + cw[1:2, :] * pltpu.roll(ev, 1, 0) + cw[2:3, :] * ev
    yd = cb * conv[HALO_ROWS:, :]

    yad_ref[0, :, 0:w] = ya.astype(out_dtype)
    yad_ref[0, :, w:2 * w] = yd.astype(out_dtype)
    new_halo = e[tm:tm + HALO_ROWS, :]
    ext_ref[0:HALO_ROWS, :] = new_halo
    state_ref[0] = new_halo


def _mixers(u_mix, halo, wbd, pscale, cw, *, tm, pos0):
    b, t, d = u_mix.shape
    w = d // 4
    return pl.pallas_call(
        functools.partial(_mixers_body, tm=tm, pos0=pos0, w=w),
        out_shape=[jax.ShapeDtypeStruct((b, t, 2 * w), BF16 if tm % 16 == 0 else F32),
                   jax.ShapeDtypeStruct((b, HALO_ROWS, 2 * w), F32)],
        grid=(b, t // tm),
        in_specs=[pl.BlockSpec((1, tm, d), lambda i, j: (i, j, 0)),
                  pl.BlockSpec((1, HALO_ROWS, 2 * w), lambda i, j: (i, 0, 0)),
                  _const_spec(wbd.shape), _const_spec(pscale.shape), _const_spec(cw.shape)],
        out_specs=[pl.BlockSpec((1, tm, 2 * w), lambda i, j: (i, j, 0)),
                   pl.BlockSpec((1, HALO_ROWS, 2 * w), lambda i, j: (i, 0, 0))],
        scratch_shapes=[pltpu.VMEM((HALO_ROWS + tm, 2 * w), F32)],
        compiler_params=_cparams("parallel", "arbitrary"),
        name="mixers",
    )(u_mix, halo, wbd, pscale, cw)


def _softmax_step(s, v, m_ref, l_ref, acc_ref, shift=None):
    n_k = s.shape[1] // LANES
    n_v = acc_ref.shape[1] // LANES
    m_prev = m_ref[...]
    m_cur = jnp.max(s, axis=1, keepdims=True)
    if shift is not None:
        m_cur = m_cur + shift
    m_new = jnp.maximum(m_prev, m_cur)
    alpha = jnp.exp(m_prev - m_new)
    sub = m_new if shift is None else m_new - shift
    p = jnp.exp(s - _rep(sub, n_k))
    l_ref[...] = alpha * l_ref[...] + jnp.sum(p, axis=1, keepdims=True)
    acc_ref[...] = _rep(alpha, n_v) * acc_ref[...] + _dot(p.astype(BF16), v)
    m_ref[...] = m_new


def _diff_lambda(lqk_ref, lam_init):
    lq = lqk_ref[...]
    a = jnp.sum(lq[0:1, :] * lq[1:2, :], axis=1, keepdims=True)
    b = jnp.sum(lq[2:3, :] * lq[3:4, :], axis=1, keepdims=True)
    return jnp.exp(a) - jnp.exp(b) + lam_init


def _diff_finish(o0, o1, lam, g, lam_init):
    o = o0 - lam * o1
    ms = jnp.mean(o * o, axis=-1, keepdims=True)
    return o * lax.rsqrt(ms + RMS_EPS) * g * (1.0 - lam_init)


def _flash_body(*refs, tq, tk, fox, lam_init):
    if fox:
        q_ref, k_ref, v_ref, fk_ref, fq_ref, o_ref, m_ref, l_ref, acc_ref = refs
    else:
        q_ref, k_ref, v_ref, lqk_ref, g_ref, o_ref, m_ref, l_ref, acc_ref = refs
    qi = pl.program_id(2)
    half = HEAD_DIM

    q = q_ref[...].astype(F32)
    lane = lax.broadcasted_iota(jnp.int32, (tq, LANES), 1)
    low = lane < half
    qs = jnp.concatenate([jnp.where(low, q, 0.0), jnp.where(low, 0.0, q)], axis=0).astype(BF16)

    m_ref[...] = jnp.full(m_ref.shape, NEG_INF, F32)
    l_ref[...] = jnp.zeros(l_ref.shape, F32)
    acc_ref[...] = jnp.zeros(acc_ref.shape, F32)

    if fox:
        fq = fq_ref[0]
        shift = jnp.concatenate([jnp.broadcast_to(fq[:, 0:1], (tq, LANES)),
                                 jnp.broadcast_to(fq[:, 1:2], (tq, LANES))], axis=0)
    else:
        shift = None

    def logits(ki):
        start = pl.multiple_of(ki * tk, tk)
        s = _dot_nt(qs, k_ref[pl.ds(start, tk), :])
        if fox:
            fk = fk_ref[0, 0, :, pl.ds(start, tk)]
            s = jnp.concatenate([s[0:tq, :] - fk[0:1, :], s[tq:, :] - fk[1:2, :]], axis=0)
        return s, v_ref[pl.ds(start, tk), :]

    def below_diagonal(ki, carry):
        s, v = logits(ki)
        _softmax_step(s, v, m_ref, l_ref, acc_ref, shift)
        return carry

    lax.fori_loop(0, qi, below_diagonal, 0)

    s, v = logits(qi)
    row = lax.broadcasted_iota(jnp.int32, (2 * tq, tk), 0)
    col = lax.broadcasted_iota(jnp.int32, (2 * tq, tk), 1)
    row = jnp.where(row >= tq, row - tq, row)
    s = jnp.where(col <= row, s, NEG_INF)
    _softmax_step(s, v, m_ref, l_ref, acc_ref, shift)

    out = acc_ref[...] / l_ref[...]
    if fox:
        o_ref[...] = jnp.where(low, out[0:tq, :], out[tq:, :]).astype(BF16)
    else:
        lam = _diff_lambda(lqk_ref, lam_init)
        o_ref[...] = _diff_finish(out[0:tq, :], out[tq:, :], lam, g_ref[...], lam_init).astype(BF16)


def _flash(q, k, v, extra, *, batch, seq, tq, tk, fox, lam_init=0.0):
    m, w = q.shape
    n_blk = w // LANES
    n_q = seq // tq
    q_spec = pl.BlockSpec((tq, LANES), lambda b, h, i: (b * n_q + i, h))
    kv_spec = pl.BlockSpec((seq, LANES), lambda b, h, i: (b, h))
    if fox:
        fk, fq = extra
        extra_specs = [pl.BlockSpec((1, 1, 2, seq), lambda b, h, i: (b, h, 0, 0)),
                       pl.BlockSpec((1, tq, 2), lambda b, h, i: (h, b * n_q + i, 0))]
    else:
        extra_specs = [_const_spec(extra[0].shape), _const_spec(extra[1].shape)]
    return pl.pallas_call(
        functools.partial(_flash_body, tq=tq, tk=tk, fox=fox, lam_init=lam_init),
        out_shape=jax.ShapeDtypeStruct((m, w), BF16),
        grid=(batch, n_blk, n_q),
        in_specs=[q_spec, kv_spec, kv_spec] + extra_specs,
        out_specs=q_spec,
        scratch_shapes=[pltpu.VMEM((2 * tq, LANES), F32), pltpu.VMEM((2 * tq, LANES), F32),
                        pltpu.VMEM((2 * tq, LANES), F32)],
        compiler_params=_cparams("parallel", "parallel", "arbitrary"),
        name="flash_fox" if fox else "flash_diff",
    )(q, k, v, *extra)


def _lane_cumsum(x, steps, stride=1):
    lane = lax.broadcasted_iota(jnp.int32, x.shape, 1)
    s = stride
    for _ in range(steps):
        x = x + jnp.where(lane >= s, pltpu.roll(x, s, 1), 0.0)
        s *= 2
    return x


def _fcum_prompt_body(lf_ref, o_ref, *, steps):
    o_ref[...] = _lane_cumsum(lf_ref[...], steps)


def _fcum_prompt(logft, *, batch, seq):
    spec = pl.BlockSpec((8, seq), lambda b: (0, b))
    return pl.pallas_call(
        functools.partial(_fcum_prompt_body, steps=(seq - 1).bit_length()),
        out_shape=jax.ShapeDtypeStruct(logft.shape, F32),
        grid=(batch,),
        in_specs=[spec],
        out_specs=spec,
        compiler_params=_cparams("parallel"),
        name="fcum_prompt",
    )(logft)


def _fcum_past_body(pt_ref, *refs, pages_per_step, n_heads):
    in_refs = refs[:pages_per_step]
    sel_ref, o_ref, rows_ref = refs[pages_per_step:]
    g = pl.program_id(1)
    depth, n_pages, width = rows_ref.shape
    for j in range(pages_per_step):
        rows_ref[:, pl.ds(g * pages_per_step + j, 1), :] = in_refs[j][:, 0, :, :]

    @pl.when(g == pl.num_programs(1) - 1)
    def _():
        x = rows_ref[...].reshape(depth * n_pages, width)
        steps = int(math.log2(width // n_heads))
        incl = _lane_cumsum(x, steps, stride=n_heads)
        lane = lax.broadcasted_iota(jnp.int32, (depth * n_pages, LANES), 1)
        tot = jnp.where(lane >= LANES - n_heads, incl[:, width - LANES:], 0.0)
        s = n_heads
        while s < LANES:
            tot = tot + pltpu.roll(tot, s, 1)
            s *= 2
        row = lax.broadcasted_iota(jnp.int32, (depth * n_pages, LANES), 0) & (n_pages - 1)
        pre = tot
        s = 1
        while s < n_pages:
            pre = pre + jnp.where(row >= s, pltpu.roll(pre, s, 0), 0.0)
            s *= 2
        pre = pre - tot
        fcum = incl + _rep(pre, width // LANES)
        hi = fcum.astype(BF16)
        r1 = fcum - hi.astype(F32)
        mid = r1.astype(BF16)
        lo = (r1 - mid.astype(F32)).astype(BF16)
        sel = sel_ref[...]
        out = (_dot(hi, sel) + _dot(mid, sel)) + _dot(lo, sel)
        o_ref[:, 0, :, :] = out.reshape(depth, n_pages, width)


def _fcum_past(cache_logf, page_table, *, pages_per_step):
    depth, n_pool, page, n_heads = cache_logf.shape
    n_seq, n_pages = page_table.shape
    width = page * n_heads
    rows = cache_logf.reshape(depth, n_pool, 1, width)
    src = np.arange(width)
    sel = np.zeros((width, width), np.float32)
    sel[src, (src % n_heads) * page + src // n_heads] = 1.0
    sel = jnp.asarray(sel, BF16)
    in_specs = [pl.BlockSpec((depth, 1, 1, width),
                             functools.partial(lambda b, g, pt, j: (0, pt[b, g * pages_per_step + j], 0, 0), j=j))
                for j in range(pages_per_step)]
    in_specs.append(pl.BlockSpec((width, width), lambda b, g, pt: (0, 0)))
    grid_spec = pltpu.PrefetchScalarGridSpec(
        num_scalar_prefetch=1,
        grid=(n_seq, n_pages // pages_per_step),
        in_specs=in_specs,
        out_specs=pl.BlockSpec((depth, 1, n_pages, width), lambda b, g, pt: (0, b, 0, 0)),
        scratch_shapes=[pltpu.VMEM((depth, n_pages, width), F32)])
    out = pl.pallas_call(
        functools.partial(_fcum_past_body, pages_per_step=pages_per_step, n_heads=n_heads),
        out_shape=jax.ShapeDtypeStruct((depth, n_seq, n_pages, width), F32),
        grid_spec=grid_spec,
        compiler_params=_cparams("parallel", "arbitrary"),
        name="fcum_past",
    )(page_table, *([rows] * pages_per_step), sel)
    return out.reshape(depth, n_seq, n_pages, n_heads, page)


def _decode_body(pt_ref, *refs, pages, page, fox, n_heads, t_new, lam_init):
    if fox:
        q_ref, knew_ref, vnew_ref, lfnew_ref, fk_ref, fklast_ref = refs[:6]
        n_fixed = 6
    else:
        q_ref, knew_ref, vnew_ref, lqk_ref, g_ref = refs[:5]
        n_fixed = 5
    k_refs = refs[n_fixed:n_fixed + pages]
    v_refs = refs[n_fixed + pages:n_fixed + 2 * pages]
    o_ref, qs_ref, m_ref, l_ref, acc_ref = refs[n_fixed + 2 * pages:]
    g = pl.program_id(1)
    width = q_ref.shape[2]
    hw = width // n_heads
    subs = hw // HEAD_DIM
    n_rows = n_heads * subs * t_new

    @pl.when(g == 0)
    def _():
        q = q_ref[0]
        lane = lax.broadcasted_iota(jnp.int32, (t_new, width), 1)
        pieces = []
        for hs in range(n_heads * subs):
            keep = (lane >= hs * HEAD_DIM) & (lane < (hs + 1) * HEAD_DIM)
            pieces.append(jnp.where(keep, q, 0.0))
        qs_ref[...] = jnp.concatenate(pieces, axis=0).astype(BF16)
        m_ref[...] = jnp.full(m_ref.shape, NEG_INF, F32)
        l_ref[...] = jnp.zeros(l_ref.shape, F32)
        acc_ref[...] = jnp.zeros(acc_ref.shape, F32)

    def head_rows(per_head):
        return jnp.concatenate([jnp.broadcast_to(per_head(h), (t_new, per_head(h).shape[1]))
                                for h in range(n_heads)], axis=0)

    if fox:
        total = fklast_ref[0, 0, 0, :, page - 1:page]
        f_new =_lane_cumsum(lfnew_ref[0], int(math.log2(t_new))) + jnp.concatenate(
            [total, jnp.zeros((8 - n_heads, 1), F32)], axis=0)
        eye = (lax.broadcasted_iota(jnp.int32, (t_new, LANES), 0)
               == lax.broadcasted_iota(jnp.int32, (t_new, LANES), 1))
        fq_col = jnp.concatenate(
            [jnp.sum(jnp.where(eye, jnp.broadcast_to(f_new[h:h + 1, :], (t_new, LANES)), 0.0),
                     axis=1, keepdims=True) for h in range(n_heads)], axis=0)
        shift = jnp.broadcast_to(fq_col, (n_rows, LANES))
    else:
        shift = None

    qs = qs_ref[...]
    k_all = jnp.concatenate([r[0, 0].astype(BF16) for r in k_refs], axis=0)
    v_all = jnp.concatenate([r[0, 0].astype(BF16) for r in v_refs], axis=0)
    s = _dot_nt(qs, k_all)
    if fox:
        s = jnp.concatenate(
            [s[:, j * page:(j + 1) * page] - head_rows(lambda h: fk_ref[0, 0, j, h:h + 1, :])
             for j in range(pages)], axis=1)
    _softmax_step(s, v_all, m_ref, l_ref, acc_ref, shift)

    @pl.when(g == pl.num_programs(1) - 1)
    def _():
        pad = jnp.zeros((LANES - t_new, width), F32)
        k_new = jnp.concatenate([knew_ref[0], pad], axis=0).astype(BF16)
        v_new = jnp.concatenate([vnew_ref[0], pad], axis=0).astype(BF16)
        s_new = _dot_nt(qs, k_new)
        if fox:
            s_new = s_new - head_rows(lambda h: f_new[h:h + 1, :])
        row = lax.broadcasted_iota(jnp.int32, (n_rows, LANES), 0) & (t_new - 1)
        col = lax.broadcasted_iota(jnp.int32, (n_rows, LANES), 1)
        s_new = jnp.where(col <= row, s_new, NEG_INF)
        _softmax_step(s_new, v_new, m_ref, l_ref, acc_ref, shift)

        out = acc_ref[...] / _rep(l_ref[...], width // LANES)
        if fox:
            lane = lax.broadcasted_iota(jnp.int32, (t_new, width), 1)
            res = jnp.zeros((t_new, width), F32)
            for h in range(n_heads):
                keep = (lane >= h * HEAD_DIM) & (lane < (h + 1) * HEAD_DIM)
                res = jnp.where(keep, out[h * t_new:(h + 1) * t_new, :], res)
            o_ref[0] = res
        else:
            lam = _diff_lambda(lqk_ref, lam_init)
            for h in range(n_heads):
                o0 = out[(2 * h) * t_new:(2 * h + 1) * t_new, h * hw:(h + 1) * hw]
                o1 = out[(2 * h + 1) * t_new:(2 * h + 2) * t_new, h * hw:(h + 1) * hw]
                o_ref[0, :, h * hw:(h + 1) * hw] = _diff_finish(o0, o1, lam, g_ref[...], lam_init)


def _decode(q, k_new, v_new, cache_k, cache_v, page_table, extra, *, layer, pages, fox, n_heads,
            lam_init=0.0):
    n_seq, t_new, width = q.shape
    depth, n_pool, page = cache_k.shape[:3]
    n_pages = page_table.shape[1]
    ck = cache_k.reshape(depth, n_pool, page, width)
    cv = cache_v.reshape(depth, n_pool, page, width)
    seq_spec = pl.BlockSpec((1, t_new, width), lambda b, g, pt: (b, 0, 0))
    if fox:
        lf_new, fk = extra
        extra_specs = [pl.BlockSpec((1, 8, LANES), lambda b, g, pt: (b, 0, 0)),
                       pl.BlockSpec((1, 1, pages, n_heads, page), lambda b, g, pt: (layer, b, g, 0, 0)),
                       pl.BlockSpec((1, 1, 1, n_heads, page), lambda b, g, pt: (layer, b, n_pages - 1, 0, 0))]
        extra = (lf_new, fk, fk)
    else:
        extra_specs = [pl.BlockSpec(extra[0].shape, lambda b, g, pt: (0, 0)),
                       pl.BlockSpec(extra[1].shape, lambda b, g, pt: (0, 0))]
    page_specs = [pl.BlockSpec((1, 1, page, width),
                               functools.partial(lambda b, g, pt, j: (layer, pt[b, g * pages + j], 0, 0), j=j))
                  for j in range(pages)]
    subs = (width // n_heads) // HEAD_DIM
    n_rows = n_heads * subs * t_new
    grid_spec = pltpu.PrefetchScalarGridSpec(
        num_scalar_prefetch=1,
        grid=(n_seq, n_pages // pages),
        in_specs=[seq_spec, seq_spec, seq_spec] + extra_specs + page_specs + page_specs,
        out_specs=seq_spec,
        scratch_shapes=[pltpu.VMEM((n_rows, width), BF16), pltpu.VMEM((n_rows, LANES), F32),
                        pltpu.VMEM((n_rows, LANES), F32), pltpu.VMEM((n_rows, width), F32)])
    return pl.pallas_call(
        functools.partial(_decode_body, pages=pages, page=page, fox=fox, n_heads=n_heads,
                          t_new=t_new, lam_init=lam_init),
        out_shape=jax.ShapeDtypeStruct((n_seq, t_new, width), F32),
        grid_spec=grid_spec,
        compiler_params=_cparams("parallel", "arbitrary"),
        name="decode_fox" if fox else "decode_diff",
    )(page_table, q, k_new, v_new, *extra, *([ck] * pages), *([cv] * pages))


def _merge_body(x_ref, yad_ref, yb_ref, yc_ref, wg_ref, wp_ref, wo_ref, g_ref, b_ref, o_ref, *, alpha):
    x = x_ref[...]
    xb = x.astype(BF16)
    d = x.shape[1]
    w = d // 4
    yad = yad_ref[...]
    ys = (yad[:, 0:w], yb_ref[...], yc_ref[...], yad[:, w:2 * w])
    merged = None
    row = 0
    for i, y in enumerate(ys):
        gate = jax.nn.sigmoid(_dot(xb, wg_ref[:, i * d:(i + 1) * d]))
        term = gate * _dot(y.astype(BF16), wp_ref[row:row + y.shape[1], :])
        merged = term if merged is None else merged + term
        row += y.shape[1]
    out = _dot(merged.astype(BF16), wo_ref[...])
    o_ref[...] = _layer_norm(alpha * x + out, g_ref[...], b_ref[...])


def _merge(x, yad, yb, yc, wg, wp, wo, g, b, *, tm, alpha):
    m, d = x.shape
    row = lambda w: pl.BlockSpec((tm, w), lambda i: (i, 0))
    return pl.pallas_call(
        functools.partial(_merge_body, alpha=alpha),
        out_shape=jax.ShapeDtypeStruct((m, d), F32),
        grid=(m // tm,),
        in_specs=[row(d), row(yad.shape[1]), row(yb.shape[1]), row(yc.shape[1]),
                  _const_spec(wg.shape), _const_spec(wp.shape), _const_spec(wo.shape),
                  _const_spec((1, d)), _const_spec((1, d))],
        out_specs=row(d),
        compiler_params=_cparams("parallel"),
        name="merge",
    )(x, yad, yb, yc, wg, wp, wo, g, b)


def _rope_tables(pos):
    half = HEAD_DIM // 2
    inv = ROPE_THETA ** (-jnp.arange(half, dtype=F32) / half)
    ang = pos.astype(F32)[:, None] * inv[None, :]
    cos, sin = jnp.cos(ang), jnp.sin(ang)
    cos_t = jnp.tile(cos, (1, LANES // half))
    sin_t = jnp.tile(jnp.concatenate([-sin, sin], axis=1), (1, LANES // HEAD_DIM))
    return cos_t, sin_t


def _pick_tile(n, target):
    t = min(n, target)
    while n % t:
        t //= 2
    return t


def kernel(x_prompt, x_sample, cache_diff_k, cache_diff_v, cache_fox_k, cache_fox_v, cache_fox_logf,
           state_pool, state_conv, page_table, w_in, b_fgate, pool_w, pool_scale, lambda_qk, subln_g,
           conv_w, w_branch, w_o, w_ffn1_gate, w_ffn1_up, w_ffn1_down, w_ffn2_gate, w_ffn2_up,
           w_ffn2_down, ln_g, ln_b):
    batch, seq, d = x_prompt.shape
    n_seq, t_new, _ = x_sample.shape
    depth = w_in.shape[0]
    page = cache_diff_k.shape[2]
    past_len = page_table.shape[1] * page
    h_diff = cache_diff_k.shape[3]
    h_fox = cache_fox_k.shape[3]
    w_pool = state_pool.shape[-1]
    w_diff, w_fox, w_conv = d // 2, d // 4, d // 4
    alpha = (2.0 * depth) ** 0.25

    sizes = (w_pool, w_diff, w_diff, w_diff, w_fox, w_fox, w_fox, h_fox, w_conv, w_conv, w_conv, 4 * d)
    offs = np.concatenate([[0], np.cumsum(sizes)])
    col = lambda i: w_in[:, :, offs[i]:offs[i + 1]]
    w_main = jnp.concatenate([col(0), col(8), col(9), col(10), col(1), col(2), col(3), col(4), col(5), col(6)],
                             axis=-1).astype(BF16)
    w_ft = jnp.pad(jnp.swapaxes(col(7), 1, 2), ((0, 0), (0, 16 - h_fox), (0, 0))).astype(BF16)
    b_f = jnp.pad(b_fgate, ((0, 0), (0, 8 - h_fox)))[:, :, None]
    w_gate = col(11).astype(BF16)
    w_br = w_branch.astype(BF16)
    w_out = w_o.astype(BF16)
    ffn = [tuple(w.astype(BF16) for w in ws) for ws in
           ((w_ffn1_gate, w_ffn1_up, w_ffn1_down), (w_ffn2_gate, w_ffn2_up, w_ffn2_down))]
    n_grp, grp = pool_w.shape[1], pool_w.shape[2]
    wbd = jnp.zeros((depth, w_pool, w_pool), F32)
    for gidx in range(n_grp):
        wbd = wbd.at[:, gidx * grp:(gidx + 1) * grp, gidx * grp:(gidx + 1) * grp].set(pool_w[:, gidx])
    wbd = wbd.astype(BF16)

    cos_p, sin_p = _rope_tables(jnp.arange(seq, dtype=jnp.int32))
    cos_s, sin_s = _rope_tables(past_len + jnp.arange(t_new, dtype=jnp.int32))
    cos_s, sin_s = jnp.tile(cos_s, (n_seq, 1)), jnp.tile(sin_s, (n_seq, 1))

    fk_past = _fcum_past(cache_fox_logf, page_table, pages_per_step=8)

    halo_p = jnp.zeros((batch, HALO_ROWS, 2 * w_pool), F32)

    tm_p = _pick_tile(batch * seq, 512)
    tm_s = n_seq * t_new
    tq = _pick_tile(seq, 256)

    xp = x_prompt.reshape(batch * seq, d)
    xs = x_sample.reshape(n_seq * t_new, d)
    p_rows, s_rows = [], []
    for l in range(depth):
        lam_init = 0.8 - 0.6 * math.exp(-0.3 * l)
        lng = lambda i: ln_g[l, i][None, :]
        lnb = lambda i: ln_b[l, i][None, :]
        halo_s = jnp.concatenate(
            [jnp.pad(state_pool[l], ((0, 0), (HALO_ROWS - state_pool.shape[2], 0), (0, 0))),
             jnp.pad(state_conv[l], ((0, 0), (HALO_ROWS - state_conv.shape[2], 0), (0, 0)))], axis=-1)

        def mixing(x, n_b, t, tm, cos_t, sin_t, halo, pos0):
            (u_mix, qd, kd32, kd16, vd32, vd16, qf, kf32, kf16, vf32, vf16, logft) = _in_proj(
                x, w_main[l], w_ft[l], b_f[l], cos_t, sin_t, tm=tm)
            yad, state = _mixers(u_mix.reshape(n_b, t, d), halo, wbd[l], pool_scale[l][None, :], conv_w[l],
                                 tm=min(t, 512), pos0=pos0)
            return (u_mix, qd, kd32, kd16, vd32, vd16, qf, kf32, kf16, vf32, vf16, logft,
                    yad.reshape(n_b * t, 2 * w_pool), state)

        xp = _ffn_ln(xp, *[w[l] for w in ffn[0]], lng(0), lnb(0), tm=tm_p, alpha=alpha)
        (_, qd, kd32, kd16, vd32, vd16, qf, kf32, kf16, vf32, vf16, logft, yad, state) = mixing(
            xp, batch, seq, tm_p, cos_p, sin_p, halo_p, 0)
        yb = _flash(qd, kd16, vd16, (lambda_qk[l], subln_g[l][None, :]), batch=batch, seq=seq, tq=tq, tk=tq,
                    fox=False, lam_init=lam_init)
        logf = logft[:h_fox].reshape(h_fox, batch, seq)
        f_cum = _fcum_prompt(logft, batch=batch, seq=seq)[:h_fox].reshape(h_fox, batch, seq)
        fk =jnp.transpose(f_cum, (1, 0, 2)).reshape(batch, h_fox // 2, 2, seq)
        fq = jnp.transpose(f_cum.reshape(h_fox // 2, 2, batch * seq), (0, 2, 1))
        yc = _flash(qf, kf16, vf16, (fk, fq), batch=batch, seq=seq, tq=tq, tk=tq, fox=True)
        xp = _merge(xp, yad, yb, yc, w_gate[l], w_br[l], w_out[l], lng(1), lnb(1), tm=tm_p, alpha=alpha)
        xp = _ffn_ln(xp, *[w[l] for w in ffn[1]], lng(2), lnb(2), tm=tm_p, alpha=alpha)
        p_rows.append((kd32.reshape(batch, seq, h_diff, -1), vd32.reshape(batch, seq, h_diff, -1),
                       kf32.reshape(batch, seq, h_fox, -1), vf32.reshape(batch, seq, h_fox, -1),
                       jnp.transpose(logf, (1, 2, 0)),
                       state[:, 1:, :w_pool], state[:, HALO_ROWS - state_conv.shape[2]:, w_pool:]))

        xs = _ffn_ln(xs, *[w[l] for w in ffn[0]], lng(0), lnb(0), tm=tm_s, alpha=alpha)
        (_, qd, kd32, kd16, vd32, vd16, qf, kf32, kf16, vf32, vf16, logft, yad, state) = mixing(
            xs, n_seq, t_new, tm_s, cos_s, sin_s, halo_s, past_len)
        seq3 = lambda a: a.reshape(n_seq, t_new, a.shape[-1])
        yb = _decode(seq3(qd.astype(F32)), seq3(kd32), seq3(vd32), cache_diff_k, cache_diff_v, page_table,
                     (lambda_qk[l], subln_g[l][None, :]), layer=l, pages=8, fox=False, n_heads=h_diff,
                     lam_init=lam_init)
        logf = logft[:h_fox].reshape(h_fox, n_seq, t_new)
        lf_new = jnp.pad(jnp.transpose(logf, (1, 0, 2)), ((0, 0), (0, 8 - h_fox), (0, LANES - t_new)))
        yc = _decode(seq3(qf.astype(F32)), seq3(kf32), seq3(vf32), cache_fox_k, cache_fox_v, page_table,
                     (lf_new, fk_past), layer=l, pages=8, fox=True, n_heads=h_fox)
        xs = _merge(xs, yad, yb.reshape(n_seq * t_new, -1), yc.reshape(n_seq * t_new, -1),
                    w_gate[l], w_br[l], w_out[l], lng(1), lnb(1), tm=tm_s, alpha=alpha)
        xs = _ffn_ln(xs, *[w[l] for w in ffn[1]], lng(2), lnb(2), tm=tm_s, alpha=alpha)
        s_rows.append((kd32.reshape(n_seq, t_new, h_diff, -1), vd32.reshape(n_seq, t_new, h_diff, -1),
                       kf32.reshape(n_seq, t_new, h_fox, -1), vf32.reshape(n_seq, t_new, h_fox, -1),
                       jnp.transpose(logf, (1, 2, 0)),
                       state[:, 1:, :w_pool], state[:, HALO_ROWS - state_conv.shape[2]:, w_pool:]))

    p_out = [jnp.stack([r[i] for r in p_rows], axis=0) for i in range(7)]
    s_out = [jnp.stack([r[i] for r in s_rows], axis=0) for i in range(7)]
    return (xp.reshape(batch, seq, d), xs.reshape(n_seq, t_new, d), *p_out, *s_out)
```

```python
import functools
import math

import jax
import jax.numpy as jnp
import numpy as np
from jax import lax
from jax.experimental import pallas as pl
from jax.experimental.pallas import tpu as pltpu

F32 = jnp.float32
BF16 = jnp.bfloat16

HEAD_DIM = 64
ROPE_THETA = 10000.0
LN_EPS = 1e-5
RMS_EPS = 1e-5
NEG_INF = -1e30
LOG2E = math.log2(math.e)
POOL_WINDOWS = (2, 4, 8, 16)
HALO_ROWS = 16
LANES = 128
VMEM_LIMIT_BYTES = 56 * 1024 * 1024


def _cparams(*sem):
    return pltpu.CompilerParams(dimension_semantics=sem, vmem_limit_bytes=VMEM_LIMIT_BYTES)


def _const_spec(shape):
    zeros = (0,) * len(shape)
    return pl.BlockSpec(shape, lambda *_: zeros, pipeline_mode=pl.Buffered(1))


def _dot(a, b):
    return jnp.dot(a, b, preferred_element_type=F32)


def _dot_nt(a, b):
    return lax.dot_general(a, b, (((1,), (1,)), ((), ())), preferred_element_type=F32)


def _rep(x, n):
    return x if n == 1 else jnp.concatenate([x] * n, axis=1)


def _layer_norm(z, g, b):
    mu = jnp.mean(z, axis=-1, keepdims=True)
    zc = z - mu
    var = jnp.mean(zc * zc, axis=-1, keepdims=True)
    return zc * lax.rsqrt(var + LN_EPS) * g + b


def _ffn_ln_body(x_ref, wg_ref, wu_ref, wd_ref, g_ref, b_ref, o_ref, acc_ref, *, chunk, alpha):
    x = x_ref[...]
    xb = x.astype(BF16)
    d_ff = wg_ref.shape[1]
    for c in range(d_ff // chunk):
        sl = slice(c * chunk, (c + 1) * chunk)
        gate = _dot(xb, wg_ref[:, sl])
        up = _dot(xb, wu_ref[:, sl])
        h = gate * jax.nn.sigmoid(gate) * up
        d = _dot(h.astype(BF16), wd_ref[sl, :])
        if c == 0:
            acc_ref[...] = d
        else:
            acc_ref[...] += d
    z = alpha * x + 0.5 * acc_ref[...]
    o_ref[...] = _layer_norm(z, g_ref[...], b_ref[...])


def _ffn_ln(x, wg, wu, wd, g, b, *, tm, alpha):
    m, d = x.shape
    d_ff = wg.shape[1]
    row = pl.BlockSpec((tm, d), lambda i: (i, 0))
    return pl.pallas_call(
        functools.partial(_ffn_ln_body, chunk=256, alpha=alpha),
        out_shape=jax.ShapeDtypeStruct((m, d), F32),
        grid=(m // tm,),
        in_specs=[row, _const_spec((d, d_ff)), _const_spec((d, d_ff)), _const_spec((d_ff, d)),
                  _const_spec((1, d)), _const_spec((1, d))],
        out_specs=row,
        scratch_shapes=[pltpu.VMEM((tm, d), F32)],
        compiler_params=_cparams("parallel"),
        name="ffn_ln",
    )(x, wg, wu, wd, g, b)


def _in_proj_body(x_ref, w_ref, wft_ref, bf_ref, cos_ref, sin_ref,
                  umix_ref, qd_ref, kd32_ref, kd16_ref, vd32_ref, vd16_ref,
                  qf_ref, kf32_ref, kf16_ref, vf32_ref, vf16_ref, logft_ref, *, d, q_scale):
    w_mix, w_diff, w_fox = d, d // 2, d // 4
    xb = x_ref[...].astype(BF16)
    tm = xb.shape[0]
    umix_ref[...] = _dot(xb, w_ref[:, 0:w_mix])

    qk = _dot(xb, w_ref[:, w_mix:w_mix + 2 * w_diff])
    cos = cos_ref[...]
    sin = sin_ref[...]
    lane = lax.broadcasted_iota(jnp.int32, (tm, LANES), 1)
    first_half = (lane & (HEAD_DIM // 2)) == 0
    n_blk = w_diff // LANES
    for j in range(2 * n_blk):
        blk = qk[:, j * LANES:(j + 1) * LANES]
        partner = jnp.where(first_half, pltpu.roll(blk, LANES - HEAD_DIM // 2, 1),
                            pltpu.roll(blk, HEAD_DIM // 2, 1))
        r = blk * cos + partner * sin
        if j < n_blk:
            qd_ref[:, j * LANES:(j + 1) * LANES] = (r * q_scale).astype(BF16)
        else:
            sl = slice((j - n_blk) * LANES, (j - n_blk + 1) * LANES)
            kd32_ref[:, sl] = r
            kd16_ref[:, sl] = r.astype(BF16)

    off = w_mix + 2 * w_diff
    vd = _dot(xb, w_ref[:, off:off + w_diff])
    vd32_ref[...] = vd
    vd16_ref[...] = vd.astype(BF16)

    off += w_diff
    fox = _dot(xb, w_ref[:, off:off + 3 * w_fox])
    qf_ref[...] = (fox[:, 0:w_fox] * q_scale).astype(BF16)
    kf = fox[:, w_fox:2 * w_fox]
    kf32_ref[...] = kf
    kf16_ref[...] = kf.astype(BF16)
    vf = fox[:, 2 * w_fox:3 * w_fox]
    vf32_ref[...] = vf
    vf16_ref[...] = vf.astype(BF16)

    ft = _dot_nt(wft_ref[...], xb)
    logft_ref[...] = jax.nn.log_sigmoid(ft[0:8, :] + bf_ref[...])


def _in_proj(x, w_main, w_ft, b_f, cos_t, sin_t, *, tm, q_scale):
    m, d = x.shape
    n_tab = cos_t.shape[0] // tm
    w_diff, w_fox = d // 2, d // 4
    row = lambda w: pl.BlockSpec((tm, w), lambda i: (i, 0))
    tab = pl.BlockSpec((tm, LANES), lambda i: (i % n_tab, 0))
    sds = jax.ShapeDtypeStruct
    out_shape = [sds((m, d), F32),
                 sds((m, w_diff), BF16), sds((m, w_diff), F32), sds((m, w_diff), BF16),
                 sds((m, w_diff), F32), sds((m, w_diff), BF16),
                 sds((m, w_fox), BF16), sds((m, w_fox), F32), sds((m, w_fox), BF16),
                 sds((m, w_fox), F32), sds((m, w_fox), BF16),
                 sds((8, m), F32)]
    out_specs = [row(d), row(w_diff), row(w_diff), row(w_diff), row(w_diff), row(w_diff),
                 row(w_fox), row(w_fox), row(w_fox), row(w_fox), row(w_fox),
                 pl.BlockSpec((8, tm), lambda i: (0, i))]
    return pl.pallas_call(
        functools.partial(_in_proj_body, d=d, q_scale=q_scale),
        out_shape=out_shape,
        grid=(m // tm,),
        in_specs=[row(d), _const_spec(w_main.shape), _const_spec(w_ft.shape), _const_spec(b_f.shape),
                  tab, tab],
        out_specs=out_specs,
        compiler_params=_cparams("parallel"),
        name="in_proj",
    )(x, w_main, w_ft, b_f, cos_t, sin_t)


def _mixers_body(u_ref, halo_ref, wbd_ref, pscale_ref, cw_ref, yad_ref, state_ref, ext_ref,
                 *, tm, pos0, w):
    out_dtype = yad_ref.dtype
    ti = pl.program_id(1)

    @pl.when(ti == 0)
    def _():
        ext_ref[0:HALO_ROWS, :] = halo_ref[0]

    u = u_ref[0]
    up = u[:, 0:w]
    cb = u[:, w:2 * w]
    ext_ref[HALO_ROWS:HALO_ROWS + tm, 0:w] = up
    ext_ref[HALO_ROWS:HALO_ROWS + tm, w:2 * w] = u[:, 2 * w:3 * w] * u[:, 3 * w:4 * w]
    e = ext_ref[...]
    ep = e[:, 0:w]
    ev = e[:, w:2 * w]

    sums = []
    b = ep
    for k in (1, 2, 4, 8):
        b = b + pltpu.roll(b, k, 0)
        sums.append(b[HALO_ROWS:, :])
    lane = lax.broadcasted_iota(jnp.int32, (tm, w), 1)
    grp = w // len(POOL_WINDOWS)
    win_sum = jnp.where(lane < grp, sums[0],
                        jnp.where(lane < 2 * grp, sums[1], jnp.where(lane < 3 * grp, sums[2], sums[3])))
    window = jnp.where(lane < grp, POOL_WINDOWS[0],
                       jnp.where(lane < 2 * grp, POOL_WINDOWS[1],
                                 jnp.where(lane < 3 * grp, POOL_WINDOWS[2], POOL_WINDOWS[3])))
    pos = pos0 + ti * tm + lax.broadcasted_iota(jnp.int32, (tm, w), 0)
    cnt = jnp.minimum(pos + 1, window).astype(F32)
    pooled = win_sum / cnt - up
    ya = _dot(pooled.astype(BF16), wbd_ref[...]) * pscale_ref[...]

    cw = cw_ref[...]
    conv = cw[0:1, :] * pltpu.roll(ev, 2, 0) + cw[1:2, :] * pltpu.roll(ev, 1, 0) + cw[2:3, :] * ev
    yd = cb * conv[HALO_ROWS:, :]

    yad_ref[0, :, 0:w] = ya.astype(out_dtype)
    yad_ref[0, :, w:2 * w] = yd.astype(out_dtype)
    new_halo = e[tm:tm + HALO_ROWS, :]
    ext_ref[0:HALO_ROWS, :] = new_halo
    state_ref[0] = new_halo


def _mixers(u_mix, halo, wbd, pscale, cw, *, tm, pos0):
    b, t, d = u_mix.shape
    w = d // 4
    return pl.pallas_call(
        functools.partial(_mixers_body, tm=tm, pos0=pos0, w=w),
        out_shape=[jax.ShapeDtypeStruct((b, t, 2 * w), BF16 if tm % 16 == 0 else F32),
                   jax.ShapeDtypeStruct((b, HALO_ROWS, 2 * w), F32)],
        grid=(b, t // tm),
        in_specs=[pl.BlockSpec((1, tm, d), lambda i, j: (i, j, 0)),
                  pl.BlockSpec((1, HALO_ROWS, 2 * w), lambda i, j: (i, 0, 0)),
                  _const_spec(wbd.shape), _const_spec(pscale.shape), _const_spec(cw.shape)],
        out_specs=[pl.BlockSpec((1, tm, 2 * w), lambda i, j: (i, j, 0)),
                   pl.BlockSpec((1, HALO_ROWS, 2 * w), lambda i, j: (i, 0, 0))],
        scratch_shapes=[pltpu.VMEM((HALO_ROWS + tm, 2 * w), F32)],
        compiler_params=_cparams("parallel", "arbitrary"),
        name="mixers",
    )(u_mix, halo, wbd, pscale, cw)


def _diff_lambda(lqk_ref, lam_init):
    lq = lqk_ref[...]
    a = jnp.sum(lq[0:1, :] * lq[1:2, :], axis=1, keepdims=True)
    b = jnp.sum(lq[2:3, :] * lq[3:4, :], axis=1, keepdims=True)
    return jnp.exp(a) - jnp.exp(b) + lam_init


def _lane_cumsum(x, steps):
    lane = lax.broadcasted_iota(jnp.int32, x.shape, 1)
    s = 1
    for _ in range(steps):
        x = x + jnp.where(lane >= s, pltpu.roll(x, s, 1), 0.0)
        s *= 2
    return x


def _flash_body(*refs, tq, tk, n_split, fox, lam_init):
    if fox:
        q_ref, k_ref, v_ref, frow_ref, fcol_ref, o_ref, vt_ref, fkc_ref, s0_ref, s1_ref, acc_ref = refs
    else:
        q_ref, k_ref, v_ref, lqk_ref, gcol_ref, o_ref, vt_ref, s0_ref, s1_ref, acc_ref = refs
    qi = pl.program_id(2)
    seq = k_ref.shape[0]
    half = HEAD_DIM
    chunk = min(seq, 512)

    @pl.when(qi == 0)
    def _():
        vt_ref[LANES:, :] = jnp.ones((vt_ref.shape[0] - LANES, seq), BF16)
        for c in range(seq // chunk):
            sl = slice(c * chunk, (c + 1) * chunk)
            vt_ref[0:LANES, sl] = v_ref[sl, :].astype(F32).T.astype(BF16)
            if fox:
                fc = fcol_ref[0, 0, sl, :] * LOG2E
                fkc_ref[0, sl, :] = jnp.broadcast_to(fc[:, 0:1], (chunk, LANES))
                fkc_ref[1, sl, :] = jnp.broadcast_to(fc[:, 1:2], (chunk, LANES))

    qt = q_ref[...].astype(F32).T
    low = lax.broadcasted_iota(jnp.int32, (LANES, tq), 0) < half
    qst = jnp.concatenate([jnp.where(low, qt, 0.0), jnp.where(low, 0.0, qt)], axis=1).astype(BF16)

    acc_ref[...] = jnp.zeros(acc_ref.shape, F32)
    q0 = pl.multiple_of(qi * tq, tq)
    cw = tq // n_split
    n_chain = 2 * n_split
    lanes = [slice(c * cw, (c + 1) * cw) for c in range(n_chain)]
    if fox:
        shift = [frow_ref[0, 0, c // n_split:c // n_split + 1, pl.ds(q0 + (c % n_split) * cw, cw)] * LOG2E
                 for c in range(n_chain)]
    n_rep = cw // LANES

    def scores(ki, s_ref):
        k = k_ref[pl.ds(pl.multiple_of(ki * tk, tk), tk), :]
        for c in range(n_chain):
            s_ref[:, lanes[c]] = _dot(k, qst[:, lanes[c]])

    def softmax_pv(ki, s_ref, ms, masked):
        start = pl.multiple_of(ki * tk, tk)
        vt = vt_ref[:, pl.ds(start, tk)]
        new = []
        for c in range(n_chain):
            st = s_ref[:, lanes[c]]
            if fox:
                st = st - _rep(fkc_ref[c // n_split, pl.ds(start, tk), :], n_rep)
            if masked:
                visible = (start + lax.broadcasted_iota(jnp.int32, (tk, cw), 0)
                           <= q0 + (c % n_split) * cw + lax.broadcasted_iota(jnp.int32, (tk, cw), 1))
                st = jnp.where(visible, st, NEG_INF)
            m_cur = jnp.max(st, axis=0, keepdims=True)
            if fox:
                m_cur = m_cur + shift[c]
            m_new = jnp.maximum(ms[c], m_cur)
            alpha = jnp.exp2(ms[c] - m_new)
            p = jnp.exp2(st - ((m_new - shift[c]) if fox else m_new))
            acc_ref[:, lanes[c]] = alpha * acc_ref[:, lanes[c]] + _dot(vt, p.astype(BF16))
            new.append(m_new)
        return tuple(new)

    def two_tiles(j, ms, masked):
        scores(2 * j + 1, s1_ref)
        ms = softmax_pv(2 * j, s0_ref, ms, masked)
        if not masked:
            scores(2 * j + 2, s0_ref)
        return softmax_pv(2 * j + 1, s1_ref, ms, masked)

    scores(0, s0_ref)
    ms = lax.fori_loop(0, qi, lambda j, c: two_tiles(j, c, False), (jnp.full((1, cw), NEG_INF, F32),) * n_chain)
    two_tiles(qi, ms, True)

    acc = acc_ref[...]
    out = acc[0:LANES, :] / acc[LANES:LANES + 1, :]
    if fox:
        res = jnp.where(low, out[:, 0:tq], out[:, tq:])
    else:
        lam = _diff_lambda(lqk_ref, lam_init)
        o = out[:, 0:tq] - lam * out[:, tq:]
        ms = jnp.mean(o * o, axis=0, keepdims=True)
        res = o * lax.rsqrt(ms + RMS_EPS) * gcol_ref[...] * (1.0 - lam_init)
    o_ref[...] = res.T.astype(BF16)


def _flash(q, k, v, extra, *, batch, seq, tq, tk, fox, lam_init=0.0):
    m, w = q.shape
    n_blk = w // LANES
    n_q = seq // tq
    q_spec = pl.BlockSpec((tq, LANES), lambda b, h, i: (b * n_q + i, h))
    kv_spec = pl.BlockSpec((seq, LANES), lambda b, h, i: (b, h))
    assert tq == 2 * tk
    ones_rows = 16
    scratch = [pltpu.VMEM((LANES + ones_rows, seq), BF16)]
    if fox:
        extra_specs = [pl.BlockSpec((1, 1, 2, seq), lambda b, h, i: (b, h, 0, 0)),
                       pl.BlockSpec((1, 1, seq, 2), lambda b, h, i: (h, b, 0, 0))]
        scratch.append(pltpu.VMEM((2, seq, LANES), F32))
    else:
        extra_specs = [_const_spec(extra[0].shape), _const_spec(extra[1].shape)]
    scratch += [pltpu.VMEM((tk, 2 * tq), F32), pltpu.VMEM((tk, 2 * tq), F32),
                pltpu.VMEM((LANES + ones_rows, 2 * tq), F32)]
    return pl.pallas_call(
        functools.partial(_flash_body, tq=tq, tk=tk, n_split=max(1, tq // 256), fox=fox, lam_init=lam_init),
        out_shape=jax.ShapeDtypeStruct((m, w), BF16),
        grid=(batch, n_blk, n_q),
        in_specs=[q_spec, kv_spec, kv_spec] + extra_specs,
        out_specs=q_spec,
        scratch_shapes=scratch,
        compiler_params=_cparams("parallel", "parallel", "arbitrary"),
        name="flash_fox" if fox else "flash_diff",
    )(q, k, v, *extra)


def _fcum_prompt_body(lf_ref, o_ref, *, steps):
    o_ref[...] = _lane_cumsum(lf_ref[...], steps)


def _fcum_prompt(logft, *, batch, seq):
    spec = pl.BlockSpec((8, seq), lambda b: (0, b))
    return pl.pallas_call(
        functools.partial(_fcum_prompt_body, steps=(seq - 1).bit_length()),
        out_shape=jax.ShapeDtypeStruct(logft.shape, F32),
        grid=(batch,),
        in_specs=[spec],
        out_specs=spec,
        compiler_params=_cparams("parallel"),
        name="fcum_prompt",
    )(logft)


def _fcum_past_body(pt_ref, *refs, pages):
    in_refs = refs[:pages]
    o_ref, x_ref, carry_ref = refs[pages:]
    depth, _, n_heads, page = in_refs[0].shape
    grp = depth * n_heads
    g = pl.program_id(1)

    @pl.when(g == 0)
    def _():
        carry_ref[...] = jnp.zeros(carry_ref.shape, F32)

    for j in range(pages):
        for d in range(depth):
            x_ref[pl.ds(j * grp + d * n_heads, n_heads), :] = in_refs[j][d, 0]
    incl = _lane_cumsum(x_ref[...], int(math.log2(page)))
    tot = jnp.broadcast_to(incl[:, page - 1:page], incl.shape)
    row = lax.broadcasted_iota(jnp.int32, incl.shape, 0)
    pre = tot
    s = grp
    while s < pages * grp:
        pre = pre + jnp.where(row >= s, pltpu.roll(pre, s, 0), 0.0)
        s *= 2
    carry = carry_ref[...]
    x_ref[...] = incl + (pre - tot) + jnp.concatenate([carry] * pages, axis=0)
    carry_ref[...] = carry + pre[(pages - 1) * grp:, :]
    for j in range(pages):
        for d in range(depth):
            o_ref[d, 0, j] = x_ref[pl.ds(j * grp + d * n_heads, n_heads), :]


def _fcum_past(logf_t, page_table, *, pages):
    depth, n_pool, n_heads, page = logf_t.shape
    n_seq, n_pages = page_table.shape
    in_specs = [pl.BlockSpec((depth, 1, n_heads, page),
                             functools.partial(lambda b, g, pt, j: (0, pt[b, g * pages + j], 0, 0), j=j))
                for j in range(pages)]
    grid_spec = pltpu.PrefetchScalarGridSpec(
        num_scalar_prefetch=1,
        grid=(n_seq, n_pages // pages),
        in_specs=in_specs,
        out_specs=pl.BlockSpec((depth, 1, pages, n_heads, page), lambda b, g, pt: (0, b, g, 0, 0)),
        scratch_shapes=[pltpu.VMEM((pages * depth * n_heads, page), F32),
                        pltpu.VMEM((depth * n_heads, page), F32)])
    return pl.pallas_call(
        functools.partial(_fcum_past_body, pages=pages),
        out_shape=jax.ShapeDtypeStruct((depth, n_seq, n_pages, n_heads, page), F32),
        grid_spec=grid_spec,
        compiler_params=_cparams("parallel", "arbitrary"),
        name="fcum_past",
    )(page_table, *([logf_t] * pages))


def _softmax_step(s, pv, m_ref, l_ref, acc_ref, shift=None):
    n_k = s.shape[1] // LANES
    n_v = acc_ref.shape[1] // LANES
    m_prev = m_ref[...]
    m_cur = jnp.max(s, axis=1, keepdims=True)
    if shift is not None:
        m_cur = m_cur + shift
    m_new = jnp.maximum(m_prev, m_cur)
    alpha = jnp.exp(m_prev - m_new)
    sub = m_new if shift is None else m_new - shift
    p = jnp.exp(s - _rep(sub, n_k))
    l_ref[...] = alpha * l_ref[...] + jnp.sum(p, axis=1, keepdims=True)
    acc_ref[...] = _rep(alpha, n_v) * acc_ref[...] + pv(p.astype(BF16))
    m_ref[...] = m_new


def _pad_rows(x, rows):
    return jnp.concatenate([x, jnp.zeros((rows - x.shape[0], x.shape[1]), x.dtype)], axis=0)


def _decode_diff_body(pt_ref, q_ref, knew_ref, vnew_ref, lqk_ref, g_ref, *refs,
                      pages, n_heads, t_new, lam_init):
    k_refs = refs[:pages]
    v_refs = refs[pages:2 * pages]
    o_ref, qs_ref, m_ref, l_ref, acc_ref = refs[2 * pages:]
    g = pl.program_id(1)
    page = k_refs[0].shape[2] // n_heads
    hw = 2 * HEAD_DIM
    rows_h = 2 * t_new

    @pl.when(g == 0)
    def _():
        q = q_ref[0]
        low = lax.broadcasted_iota(jnp.int32, (t_new, hw), 1) < HEAD_DIM
        pieces = []
        for h in range(n_heads):
            qh = q[:, h * hw:(h + 1) * hw]
            pieces += [jnp.where(low, qh, 0.0), jnp.where(low, 0.0, qh)]
        qs_ref[...] = jnp.concatenate(pieces, axis=0).astype(BF16)
        m_ref[...] = jnp.full(m_ref.shape, NEG_INF, F32)
        l_ref[...] = jnp.zeros(l_ref.shape, F32)
        acc_ref[...] = jnp.zeros(acc_ref.shape, F32)

    def head_rows(page_refs, h):
        return jnp.concatenate([r[0, 0, pl.ds(h, page, stride=n_heads), :].astype(BF16) for r in page_refs],
                               axis=0)

    def per_head(fn):
        return jnp.concatenate([fn(h, slice(h * rows_h, (h + 1) * rows_h)) for h in range(n_heads)], axis=0)

    qs = qs_ref[...]
    vs = [head_rows(v_refs, h) for h in range(n_heads)]
    s = per_head(lambda h, rows: _dot_nt(qs[rows, :], head_rows(k_refs, h)))
    _softmax_step(s, lambda p: per_head(lambda h, rows: _dot(p[rows, :], vs[h])), m_ref, l_ref, acc_ref)

    @pl.when(g == pl.num_programs(1) - 1)
    def _():
        k_new = knew_ref[0]
        v_new = vnew_ref[0]
        kn = [_pad_rows(k_new[:, h * hw:(h + 1) * hw], LANES).astype(BF16) for h in range(n_heads)]
        vn = [_pad_rows(v_new[:, h * hw:(h + 1) * hw], LANES).astype(BF16) for h in range(n_heads)]
        s_new = per_head(lambda h, rows: _dot_nt(qs[rows, :], kn[h]))
        row = lax.broadcasted_iota(jnp.int32, s_new.shape, 0) & (t_new - 1)
        col = lax.broadcasted_iota(jnp.int32, s_new.shape, 1)
        s_new = jnp.where(col <= row, s_new, NEG_INF)
        _softmax_step(s_new, lambda p: per_head(lambda h, rows: _dot(p[rows, :], vn[h])), m_ref, l_ref, acc_ref)

        out = acc_ref[...] / l_ref[...]
        lam = _diff_lambda(lqk_ref, lam_init)
        for h in range(n_heads):
            o = out[h * rows_h:h * rows_h + t_new, :] - lam * out[h * rows_h + t_new:(h + 1) * rows_h, :]
            ms = jnp.mean(o * o, axis=-1, keepdims=True)
            o_ref[0, :, h * hw:(h + 1) * hw] = o * lax.rsqrt(ms + RMS_EPS) * g_ref[...] * (1.0 - lam_init)


def _decode_diff(q, k_new, v_new, cache_k, cache_v, page_table, lqk, g, *, layer, pages, lam_init):
    n_seq, t_new, width = q.shape
    n_heads = width // (2 * HEAD_DIM)
    n_pages = page_table.shape[1]
    rows = cache_k.shape[2]
    seq_spec = pl.BlockSpec((1, t_new, width), lambda b, g_, pt: (b, 0, 0))
    page_specs = [pl.BlockSpec((1, 1, rows, LANES),
                               functools.partial(lambda b, g_, pt, j: (layer, pt[b, g_ * pages + j], 0, 0), j=j))
                  for j in range(pages)]
    n_rows = n_heads * 2 * t_new
    grid_spec = pltpu.PrefetchScalarGridSpec(
        num_scalar_prefetch=1,
        grid=(n_seq, n_pages // pages),
        in_specs=[seq_spec, seq_spec, seq_spec,
                  pl.BlockSpec(lqk.shape, lambda b, g_, pt: (0, 0)),
                  pl.BlockSpec(g.shape, lambda b, g_, pt: (0, 0))] + page_specs + page_specs,
        out_specs=seq_spec,
        scratch_shapes=[pltpu.VMEM((n_rows, LANES), BF16), pltpu.VMEM((n_rows, LANES), F32),
                        pltpu.VMEM((n_rows, LANES), F32), pltpu.VMEM((n_rows, LANES), F32)])
    return pl.pallas_call(
        functools.partial(_decode_diff_body, pages=pages, n_heads=n_heads, t_new=t_new, lam_init=lam_init),
        out_shape=jax.ShapeDtypeStruct((n_seq, t_new, width), F32),
        grid_spec=grid_spec,
        compiler_params=_cparams("parallel", "arbitrary"),
        name="decode_diff",
    )(page_table, q, k_new, v_new, lqk, g, *([cache_k] * pages), *([cache_v] * pages))


def _decode_fox_body(pt_ref, q_ref, knew_ref, vnew_ref, lfnew_ref, fk_ref, fklast_ref, *refs,
                     pages, n_heads, t_new):
    kt_refs = refs[:pages]
    vt_refs = refs[pages:2 * pages]
    o_ref, qs_ref, m_ref, l_ref, acc_ref = refs[2 * pages:]
    g = pl.program_id(1)
    width = q_ref.shape[2]
    page = kt_refs[0].shape[3]
    n_rows = n_heads * t_new

    @pl.when(g == 0)
    def _():
        q = q_ref[0]
        lane = lax.broadcasted_iota(jnp.int32, (t_new, width), 1)
        qs_ref[...] = jnp.concatenate(
            [jnp.where((lane >= h * HEAD_DIM) & (lane < (h + 1) * HEAD_DIM), q, 0.0) for h in range(n_heads)],
            axis=0).astype(BF16)
        m_ref[...] = jnp.full(m_ref.shape, NEG_INF, F32)
        l_ref[...] = jnp.zeros(l_ref.shape, F32)
        acc_ref[...] = jnp.zeros(acc_ref.shape, F32)

    def head_rows(per_head):
        return jnp.concatenate([jnp.broadcast_to(per_head(h), (t_new, per_head(h).shape[1]))
                                for h in range(n_heads)], axis=0)

    total = fklast_ref[0, 0, 0, :, page - 1:page]
    f_new = _lane_cumsum(lfnew_ref[0], int(math.log2(t_new))) + jnp.concatenate(
        [total, jnp.zeros((8 - n_heads, 1), F32)], axis=0)
    eye = (lax.broadcasted_iota(jnp.int32, (t_new, LANES), 0)
           == lax.broadcasted_iota(jnp.int32, (t_new, LANES), 1))
    fq_col = jnp.concatenate(
        [jnp.sum(jnp.where(eye, jnp.broadcast_to(f_new[h:h + 1, :], (t_new, LANES)), 0.0),
                 axis=1, keepdims=True) for h in range(n_heads)], axis=0)
    shift = jnp.broadcast_to(fq_col, (n_rows, LANES))

    qs = qs_ref[...]
    kt = jnp.concatenate([r[0, 0].astype(BF16) for r in kt_refs], axis=1)
    vt = jnp.concatenate([r[0, 0].astype(BF16) for r in vt_refs], axis=1)
    s = _dot(qs, kt)
    s = s - jnp.concatenate([head_rows(lambda h: fk_ref[0, 0, j, h:h + 1, :]) for j in range(pages)], axis=1)
    _softmax_step(s, lambda p: _dot_nt(p, vt), m_ref, l_ref, acc_ref, shift)

    @pl.when(g == pl.num_programs(1) - 1)
    def _():
        k_new = _pad_rows(knew_ref[0], LANES).astype(BF16)
        v_new = _pad_rows(vnew_ref[0], LANES).astype(BF16)
        s_new = _dot_nt(qs, k_new) - head_rows(lambda h: f_new[h:h + 1, :])
        row = lax.broadcasted_iota(jnp.int32, s_new.shape, 0) & (t_new - 1)
        col = lax.broadcasted_iota(jnp.int32, s_new.shape, 1)
        s_new = jnp.where(col <= row, s_new, NEG_INF)
        _softmax_step(s_new, lambda p: _dot(p, v_new), m_ref, l_ref, acc_ref, shift)

        out = acc_ref[...] / _rep(l_ref[...], width // LANES)
        lane = lax.broadcasted_iota(jnp.int32, (t_new, width), 1)
        res = jnp.zeros((t_new, width), F32)
        for h in range(n_heads):
            keep = (lane >= h * HEAD_DIM) & (lane < (h + 1) * HEAD_DIM)
            res = jnp.where(keep, out[h * t_new:(h + 1) * t_new, :], res)
        o_ref[0] = res


def _decode_fox(q, k_new, v_new, cache_kt, cache_vt, page_table, lf_new, fk, *, layer, pages):
    n_seq, t_new, width = q.shape
    n_heads = width // HEAD_DIM
    page = cache_kt.shape[3]
    n_pages = page_table.shape[1]
    seq_spec = pl.BlockSpec((1, t_new, width), lambda b, g, pt: (b, 0, 0))
    page_specs = [pl.BlockSpec((1, 1, width, page),
                               functools.partial(lambda b, g, pt, j: (layer, pt[b, g * pages + j], 0, 0), j=j))
                  for j in range(pages)]
    n_rows = n_heads * t_new
    grid_spec = pltpu.PrefetchScalarGridSpec(
        num_scalar_prefetch=1,
        grid=(n_seq, n_pages // pages),
        in_specs=[seq_spec, seq_spec, seq_spec,
                  pl.BlockSpec((1, 8, LANES), lambda b, g, pt: (b, 0, 0)),
                  pl.BlockSpec((1, 1, pages, n_heads, page), lambda b, g, pt: (layer, b, g, 0, 0)),
                  pl.BlockSpec((1, 1, 1, n_heads, page), lambda b, g, pt: (layer, b, n_pages - 1, 0, 0))]
        + page_specs + page_specs,
        out_specs=seq_spec,
        scratch_shapes=[pltpu.VMEM((n_rows, width), BF16), pltpu.VMEM((n_rows, LANES), F32),
                        pltpu.VMEM((n_rows, LANES), F32), pltpu.VMEM((n_rows, width), F32)])
    return pl.pallas_call(
        functools.partial(_decode_fox_body, pages=pages, n_heads=n_heads, t_new=t_new),
        out_shape=jax.ShapeDtypeStruct((n_seq, t_new, width), F32),
        grid_spec=grid_spec,
        compiler_params=_cparams("parallel", "arbitrary"),
        name="decode_fox",
    )(page_table, q, k_new, v_new, lf_new, fk, fk, *([cache_kt] * pages), *([cache_vt] * pages))


def _merge_body(x_ref, yad_ref, yb_ref, yc_ref, wg_ref, wp_ref, wo_ref, g_ref, b_ref, o_ref, *, alpha):
    x = x_ref[...]
    xb = x.astype(BF16)
    d = x.shape[1]
    w = d // 4
    yad = yad_ref[...]
    ys = (yad[:, 0:w], yb_ref[...], yc_ref[...], yad[:, w:2 * w])
    merged = None
    row = 0
    for i, y in enumerate(ys):
        gate = jax.nn.sigmoid(_dot(xb, wg_ref[:, i * d:(i + 1) * d]))
        term = gate * _dot(y.astype(BF16), wp_ref[row:row + y.shape[1], :])
        merged = term if merged is None else merged + term
        row += y.shape[1]
    out = _dot(merged.astype(BF16), wo_ref[...])
    o_ref[...] = _layer_norm(alpha * x + out, g_ref[...], b_ref[...])


def _merge(x, yad, yb, yc, wg, wp, wo, g, b, *, tm, alpha):
    m, d = x.shape
    row = lambda w: pl.BlockSpec((tm, w), lambda i: (i, 0))
    return pl.pallas_call(
        functools.partial(_merge_body, alpha=alpha),
        out_shape=jax.ShapeDtypeStruct((m, d), F32),
        grid=(m // tm,),
        in_specs=[row(d), row(yad.shape[1]), row(yb.shape[1]), row(yc.shape[1]),
                  _const_spec(wg.shape), _const_spec(wp.shape), _const_spec(wo.shape),
                  _const_spec((1, d)), _const_spec((1, d))],
        out_specs=row(d),
        compiler_params=_cparams("parallel"),
        name="merge",
    )(x, yad, yb, yc, wg, wp, wo, g, b)


def _rope_tables(pos):
    half = HEAD_DIM // 2
    inv = ROPE_THETA ** (-jnp.arange(half, dtype=F32) / half)
    ang = pos.astype(F32)[:, None] * inv[None, :]
    cos, sin = jnp.cos(ang), jnp.sin(ang)
    cos_t = jnp.tile(cos, (1, LANES // half))
    sin_t = jnp.tile(jnp.concatenate([-sin, sin], axis=1), (1, LANES // HEAD_DIM))
    return cos_t, sin_t


def _pick_tile(n, target):
    t = min(n, target)
    while n % t:
        t //= 2
    return t


def kernel(x_prompt, x_sample, cache_diff_k, cache_diff_v, cache_fox_k, cache_fox_v, cache_fox_logf,
           state_pool, state_conv, page_table, w_in, b_fgate, pool_w, pool_scale, lambda_qk, subln_g,
           conv_w, w_branch, w_o, w_ffn1_gate, w_ffn1_up, w_ffn1_down, w_ffn2_gate, w_ffn2_up,
           w_ffn2_down, ln_g, ln_b):
    batch, seq, d = x_prompt.shape
    n_seq, t_new, _ = x_sample.shape
    depth = w_in.shape[0]
    n_pool, page = cache_diff_k.shape[1:3]
    past_len = page_table.shape[1] * page
    h_diff = cache_diff_k.shape[3]
    h_fox = cache_fox_k.shape[3]
    w_pool = state_pool.shape[-1]
    w_diff, w_fox, w_conv = d // 2, d // 4, d // 4
    alpha = (2.0 * depth) ** 0.25

    sizes = (w_pool, w_diff, w_diff, w_diff, w_fox, w_fox, w_fox, h_fox, w_conv, w_conv, w_conv, 4 * d)
    offs = np.concatenate([[0], np.cumsum(sizes)])
    col = lambda i: w_in[:, :, offs[i]:offs[i + 1]]
    w_main = jnp.concatenate([col(0), col(8), col(9), col(10), col(1), col(2), col(3), col(4), col(5), col(6)],
                             axis=-1).astype(BF16)
    w_ft = jnp.pad(jnp.swapaxes(col(7), 1, 2), ((0, 0), (0, 16 - h_fox), (0, 0))).astype(BF16)
    b_f = jnp.pad(b_fgate, ((0, 0), (0, 8 - h_fox)))[:, :, None]
    w_gate = col(11).astype(BF16)
    w_br = w_branch.astype(BF16)
    w_out = w_o.astype(BF16)
    ffn = [tuple(w.astype(BF16) for w in ws) for ws in
           ((w_ffn1_gate, w_ffn1_up, w_ffn1_down), (w_ffn2_gate, w_ffn2_up, w_ffn2_down))]
    n_grp, grp = pool_w.shape[1], pool_w.shape[2]
    wbd = jnp.zeros((depth, w_pool, w_pool), F32)
    for gidx in range(n_grp):
        wbd = wbd.at[:, gidx * grp:(gidx + 1) * grp, gidx * grp:(gidx + 1) * grp].set(pool_w[:, gidx])
    wbd = wbd.astype(BF16)

    cos_p, sin_p = _rope_tables(jnp.arange(seq, dtype=jnp.int32))
    cos_s, sin_s = _rope_tables(past_len + jnp.arange(t_new, dtype=jnp.int32))
    cos_s, sin_s = jnp.tile(cos_s, (n_seq, 1)), jnp.tile(sin_s, (n_seq, 1))

    ck_diff = cache_diff_k.reshape(depth, n_pool, page * h_diff, 2 * HEAD_DIM)
    cv_diff = cache_diff_v.reshape(depth, n_pool, page * h_diff, 2 * HEAD_DIM)
    ck_fox = jnp.transpose(cache_fox_k, (0, 1, 3, 4, 2)).reshape(depth, n_pool, h_fox * HEAD_DIM, page)
    cv_fox = jnp.transpose(cache_fox_v, (0, 1, 3, 4, 2)).reshape(depth, n_pool, h_fox * HEAD_DIM, page)
    pages = _pick_tile(page_table.shape[1], 16)
    fk_past = _fcum_past(jnp.transpose(cache_fox_logf, (0, 1, 3, 2)), page_table, pages=min(pages, 8))

    halo_p = jnp.zeros((batch, HALO_ROWS, 2 * w_pool), F32)

    tm_p = _pick_tile(batch * seq, 512)
    tm_s = n_seq * t_new
    tq = _pick_tile(seq, 512)
    tk = tq // 2

    xp = x_prompt.reshape(batch * seq, d)
    xs = x_sample.reshape(n_seq * t_new, d)
    p_rows, s_rows = [], []
    for l in range(depth):
        lam_init = 0.8 - 0.6 * math.exp(-0.3 * l)
        lng = lambda i: ln_g[l, i][None, :]
        lnb = lambda i: ln_b[l, i][None, :]
        halo_s = jnp.concatenate(
            [jnp.pad(state_pool[l], ((0, 0), (HALO_ROWS - state_pool.shape[2], 0), (0, 0))),
             jnp.pad(state_conv[l], ((0, 0), (HALO_ROWS - state_conv.shape[2], 0), (0, 0)))], axis=-1)

        def mixing(x, n_b, t, tm, cos_t, sin_t, halo, pos0, q_scale):
            (u_mix, qd, kd32, kd16, vd32, vd16, qf, kf32, kf16, vf32, vf16, logft) = _in_proj(
                x, w_main[l], w_ft[l], b_f[l], cos_t, sin_t, tm=tm, q_scale=q_scale)
            yad, state = _mixers(u_mix.reshape(n_b, t, d), halo, wbd[l], pool_scale[l][None, :], conv_w[l],
                                 tm=min(t, 512), pos0=pos0)
            return (u_mix, qd, kd32, kd16, vd32, vd16, qf, kf32, kf16, vf32, vf16, logft,
                    yad.reshape(n_b * t, 2 * w_pool), state)

        xp = _ffn_ln(xp, *[w[l] for w in ffn[0]], lng(0), lnb(0), tm=tm_p, alpha=alpha)
        (_, qd, kd32, kd16, vd32, vd16, qf, kf32, kf16, vf32, vf16, logft, yad, state) = mixing(
            xp, batch, seq, tm_p, cos_p, sin_p, halo_p, 0, LOG2E / math.sqrt(HEAD_DIM))
        yb = _flash(qd, kd16, vd16, (lambda_qk[l], subln_g[l][:, None]), batch=batch, seq=seq, tq=tq, tk=tk,
                    fox=False, lam_init=lam_init)
        logf = logft[:h_fox].reshape(h_fox, batch, seq)
        f_cum = _fcum_prompt(logft, batch=batch, seq=seq)[:h_fox].reshape(h_fox // 2, 2, batch, seq)
        f_row = jnp.transpose(f_cum, (2, 0, 1, 3))
        f_col = jnp.transpose(f_cum, (0, 2, 3, 1))
        yc = _flash(qf, kf16, vf16, (f_row, f_col), batch=batch, seq=seq, tq=tq, tk=tk, fox=True)
        xp = _merge(xp, yad, yb, yc, w_gate[l], w_br[l], w_out[l], lng(1), lnb(1), tm=tm_p, alpha=alpha)
        xp = _ffn_ln(xp, *[w[l] for w in ffn[1]], lng(2), lnb(2), tm=tm_p, alpha=alpha)
        p_rows.append((kd32.reshape(batch, seq, h_diff, -1), vd32.reshape(batch, seq, h_diff, -1),
                       kf32.reshape(batch, seq, h_fox, -1), vf32.reshape(batch, seq, h_fox, -1),
                       jnp.transpose(logf, (1, 2, 0)),
                       state[:, 1:, :w_pool], state[:, HALO_ROWS - state_conv.shape[2]:, w_pool:]))

        xs = _ffn_ln(xs, *[w[l] for w in ffn[0]], lng(0), lnb(0), tm=tm_s, alpha=alpha)
        (_, qd, kd32, kd16, vd32, vd16, qf, kf32, kf16, vf32, vf16, logft, yad, state) = mixing(
            xs, n_seq, t_new, tm_s, cos_s, sin_s, halo_s, past_len, 1.0 / math.sqrt(HEAD_DIM))
        seq3 = lambda a: a.reshape(n_seq, t_new, a.shape[-1])
        yb = _decode_diff(seq3(qd.astype(F32)), seq3(kd32), seq3(vd32), ck_diff, cv_diff, page_table,
                          lambda_qk[l], subln_g[l][None, :], layer=l, pages=pages, lam_init=lam_init)
        logf = logft[:h_fox].reshape(h_fox, n_seq, t_new)
        lf_new = jnp.pad(jnp.transpose(logf, (1, 0, 2)), ((0, 0), (0, 8 - h_fox), (0, LANES - t_new)))
        yc = _decode_fox(seq3(qf.astype(F32)), seq3(kf32), seq3(vf32), ck_fox, cv_fox, page_table,
                         lf_new, fk_past, layer=l, pages=pages)
        xs = _merge(xs, yad, yb.reshape(n_seq * t_new, -1), yc.reshape(n_seq * t_new, -1),
                    w_gate[l], w_br[l], w_out[l], lng(1), lnb(1), tm=tm_s, alpha=alpha)
        xs = _ffn_ln(xs, *[w[l] for w in ffn[1]], lng(2), lnb(2), tm=tm_s, alpha=alpha)
        s_rows.append((kd32.reshape(n_seq, t_new, h_diff, -1), vd32.reshape(n_seq, t_new, h_diff, -1),
                       kf32.reshape(n_seq, t_new, h_fox, -1), vf32.reshape(n_seq, t_new, h_fox, -1),
                       jnp.transpose(logf, (1, 2, 0)),
                       state[:, 1:, :w_pool], state[:, HALO_ROWS - state_conv.shape[2]:, w_pool:]))

    p_out = [jnp.stack([r[i] for r in p_rows], axis=0) for i in range(7)]
    s_out = [jnp.stack([r[i] for r in s_rows], axis=0) for i in range(7)]
    return (xp.reshape(batch, seq, d), xs.reshape(n_seq, t_new, d), *p_out, *s_out)
```

```python
import functools
import math

import jax
import jax.numpy as jnp
import numpy as np
from jax import lax
from jax.experimental import pallas as pl
from jax.experimental.pallas import tpu as pltpu

F32 = jnp.float32
BF16 = jnp.bfloat16

HEAD_DIM = 64
ROPE_THETA = 10000.0
LN_EPS = 1e-5
RMS_EPS = 1e-5
NEG_INF = -1e30
LOG2E = math.log2(math.e)
POOL_WINDOWS = (2, 4, 8, 16)
HALO_ROWS = 16
LANES = 128
VMEM_LIMIT_BYTES = 56 * 1024 * 1024


def _cparams(*sem):
    return pltpu.CompilerParams(dimension_semantics=sem, vmem_limit_bytes=VMEM_LIMIT_BYTES)


def _const_spec(shape):
    zeros = (0,) * len(shape)
    return pl.BlockSpec(shape, lambda *_: zeros, pipeline_mode=pl.Buffered(1))


def _dot(a, b):
    return jnp.dot(a, b, preferred_element_type=F32)


def _dot_nt(a, b):
    return lax.dot_general(a, b, (((1,), (1,)), ((), ())), preferred_element_type=F32)


def _rep(x, n):
    return x if n == 1 else jnp.concatenate([x] * n, axis=1)


def _layer_norm(z, g, b):
    mu = jnp.mean(z, axis=-1, keepdims=True)
    zc = z - mu
    var = jnp.mean(zc * zc, axis=-1, keepdims=True)
    return zc * lax.rsqrt(var + LN_EPS) * g + b


def _ffn_ln_body(x_ref, wg_ref, wu_ref, wd_ref, g_ref, b_ref, o_ref, acc_ref, *, chunk, alpha):
    x = x_ref[...]
    xb = x.astype(BF16)
    d_ff = wg_ref.shape[1]
    for c in range(d_ff // chunk):
        sl = slice(c * chunk, (c + 1) * chunk)
        gate = _dot(xb, wg_ref[:, sl])
        up = _dot(xb, wu_ref[:, sl])
        h = gate * jax.nn.sigmoid(gate) * up
        d = _dot(h.astype(BF16), wd_ref[sl, :])
        if c == 0:
            acc_ref[...] = d
        else:
            acc_ref[...] += d
    z = alpha * x + 0.5 * acc_ref[...]
    o_ref[...] = _layer_norm(z, g_ref[...], b_ref[...])


def _ffn_ln(x, wg, wu, wd, g, b, *, tm, alpha):
    m, d = x.shape
    d_ff = wg.shape[1]
    row = pl.BlockSpec((tm, d), lambda i: (i, 0))
    return pl.pallas_call(
        functools.partial(_ffn_ln_body, chunk=256, alpha=alpha),
        out_shape=jax.ShapeDtypeStruct((m, d), F32),
        grid=(m // tm,),
        in_specs=[row, _const_spec((d, d_ff)), _const_spec((d, d_ff)), _const_spec((d_ff, d)),
                  _const_spec((1, d)), _const_spec((1, d))],
        out_specs=row,
        scratch_shapes=[pltpu.VMEM((tm, d), F32)],
        compiler_params=_cparams("parallel"),
        name="ffn_ln",
    )(x, wg, wu, wd, g, b)


def _in_proj_body(x_ref, w_ref, wft_ref, bf_ref, cos_ref, sin_ref, *refs, d, q_scale, stacked):
    if stacked:
        refs = refs[4:]
    (umix_ref, qd_ref, kd16_ref, vd16_ref, qf_ref, kf16_ref, vf16_ref, logft_ref,
     kd32_ref, vd32_ref, kf32_ref, vf32_ref) = refs
    w_mix, w_diff, w_fox = d, d // 2, d // 4
    n_blk = w_diff // LANES
    xb = x_ref[...].astype(BF16)
    tm = xb.shape[0]

    def put_diff(ref, j, val):
        if stacked:
            ref[0, pl.ds(j, tm, stride=n_blk), :] = val
        else:
            ref[:, j * LANES:(j + 1) * LANES] = val

    def put_fox(ref, val):
        if stacked:
            ref[0, 0] = val.T
        else:
            ref[...] = val

    umix_ref[...] = _dot(xb, w_ref[:, 0:w_mix])

    qk = _dot(xb, w_ref[:, w_mix:w_mix + 2 * w_diff])
    cos = cos_ref[...]
    sin = sin_ref[...]
    lane = lax.broadcasted_iota(jnp.int32, (tm, LANES), 1)
    first_half = (lane & (HEAD_DIM // 2)) == 0
    for j in range(2 * n_blk):
        blk = qk[:, j * LANES:(j + 1) * LANES]
        partner = jnp.where(first_half, pltpu.roll(blk, LANES - HEAD_DIM // 2, 1),
                            pltpu.roll(blk, HEAD_DIM // 2, 1))
        r = blk * cos + partner * sin
        if j < n_blk:
            qd_ref[:, j * LANES:(j + 1) * LANES] = (r * q_scale).astype(BF16)
        else:
            put_diff(kd32_ref, j - n_blk, r)
            kd16_ref[:, (j - n_blk) * LANES:(j - n_blk + 1) * LANES] = r.astype(BF16)

    off = w_mix + 2 * w_diff
    vd = _dot(xb, w_ref[:, off:off + w_diff])
    for j in range(n_blk):
        put_diff(vd32_ref, j, vd[:, j * LANES:(j + 1) * LANES])
    vd16_ref[...] = vd.astype(BF16)

    off += w_diff
    fox = _dot(xb, w_ref[:, off:off + 3 * w_fox])
    qf_ref[...] = (fox[:, 0:w_fox] * q_scale).astype(BF16)
    kf = fox[:, w_fox:2 * w_fox]
    put_fox(kf32_ref, kf)
    kf16_ref[...] = kf.astype(BF16)
    vf = fox[:, 2 * w_fox:3 * w_fox]
    put_fox(vf32_ref, vf)
    vf16_ref[...] = vf.astype(BF16)

    ft = _dot_nt(wft_ref[...], xb)
    logft_ref[...] = jax.nn.log_sigmoid(ft[0:8, :] + bf_ref[...])


def _in_proj(x, w_main, w_ft, b_f, cos_t, sin_t, *, tm, q_scale, stacked=None):
    m, d = x.shape
    n_tab = cos_t.shape[0] // tm
    w_diff, w_fox = d // 2, d // 4
    n_blk = w_diff // LANES
    row = lambda w: pl.BlockSpec((tm, w), lambda i: (i, 0))
    tab = pl.BlockSpec((tm, LANES), lambda i: (i % n_tab, 0))
    sds = jax.ShapeDtypeStruct
    out_shape = [sds((m, d), F32), sds((m, w_diff), BF16), sds((m, w_diff), BF16), sds((m, w_diff), BF16),
                 sds((m, w_fox), BF16), sds((m, w_fox), BF16), sds((m, w_fox), BF16), sds((8, m), F32)]
    out_specs = [row(d), row(w_diff), row(w_diff), row(w_diff), row(w_fox), row(w_fox), row(w_fox),
                 pl.BlockSpec((8, tm), lambda i: (0, i))]
    in_specs = [row(d), _const_spec(w_main.shape), _const_spec(w_ft.shape), _const_spec(b_f.shape), tab, tab]
    args = [x, w_main, w_ft, b_f, cos_t, sin_t]
    aliases = {}
    if stacked is None:
        out_shape += [sds((m, w_diff), F32), sds((m, w_diff), F32), sds((m, w_fox), F32), sds((m, w_fox), F32)]
        out_specs += [row(w_diff), row(w_diff), row(w_fox), row(w_fox)]
    else:
        layer, bufs = stacked[0], list(stacked[1:])
        diff_spec = pl.BlockSpec((1, tm * n_blk, LANES), lambda i: (layer, i, 0))
        fox_spec = pl.BlockSpec((1, 1, w_fox, tm), lambda i: (layer, i // n_tab, 0, i % n_tab))
        aliases = {len(args) + k: len(out_shape) + k for k in range(len(bufs))}
        out_shape += [sds(b.shape, b.dtype) for b in bufs]
        out_specs += [diff_spec, diff_spec, fox_spec, fox_spec]
        in_specs += [pl.BlockSpec(memory_space=pl.ANY)] * len(bufs)
        args += bufs
    return pl.pallas_call(
        functools.partial(_in_proj_body, d=d, q_scale=q_scale, stacked=stacked is not None),
        out_shape=out_shape,
        grid=(m // tm,),
        in_specs=in_specs,
        out_specs=out_specs,
        input_output_aliases=aliases,
        compiler_params=_cparams("parallel"),
        name="in_proj",
    )(*args)


def _mixers_body(u_ref, halo_ref, wbd_ref, pscale_ref, cw_ref, yad_ref, state_ref, ext_ref,
                 *, tm, pos0, w):
    out_dtype = yad_ref.dtype
    ti = pl.program_id(1)

    @pl.when(ti == 0)
    def _():
        ext_ref[0:HALO_ROWS, :] = halo_ref[0]

    u = u_ref[0]
    up = u[:, 0:w]
    cb = u[:, w:2 * w]
    ext_ref[HALO_ROWS:HALO_ROWS + tm, 0:w] = up
    ext_ref[HALO_ROWS:HALO_ROWS + tm, w:2 * w] = u[:, 2 * w:3 * w] * u[:, 3 * w:4 * w]
    e = ext_ref[...]
    ep = e[:, 0:w]
    ev = e[:, w:2 * w]

    sums = []
    b = ep
    for k in (1, 2, 4, 8):
        b = b + pltpu.roll(b, k, 0)
        sums.append(b[HALO_ROWS:, :])
    lane = lax.broadcasted_iota(jnp.int32, (tm, w), 1)
    grp = w // len(POOL_WINDOWS)
    win_sum = jnp.where(lane < grp, sums[0],
                        jnp.where(lane < 2 * grp, sums[1], jnp.where(lane < 3 * grp, sums[2], sums[3])))
    window = jnp.where(lane < grp, POOL_WINDOWS[0],
                       jnp.where(lane < 2 * grp, POOL_WINDOWS[1],
                                 jnp.where(lane < 3 * grp, POOL_WINDOWS[2], POOL_WINDOWS[3])))
    pos = pos0 + ti * tm + lax.broadcasted_iota(jnp.int32, (tm, w), 0)
    cnt = jnp.minimum(pos + 1, window).astype(F32)
    pooled = win_sum / cnt - up
    ya = _dot(pooled.astype(BF16), wbd_ref[...]) * pscale_ref[...]

    cw = cw_ref[...]
    conv = cw[0:1, :] * pltpu.roll(ev, 2, 0) + cw[1:2, :] * pltpu.roll(ev, 1, 0) + cw[2:3, :] * ev
    yd = cb * conv[HALO_ROWS:, :]

    yad_ref[0, :, 0:w] = ya.astype(out_dtype)
    yad_ref[0, :, w:2 * w] = yd.astype(out_dtype)
    new_halo = e[tm:tm + HALO_ROWS, :]
    ext_ref[0:HALO_ROWS, :] = new_halo
    state_ref[0] = new_halo


def _mixers(u_mix, halo, wbd, pscale, cw, *, tm, pos0):
    b, t, d = u_mix.shape
    w = d // 4
    return pl.pallas_call(
        functools.partial(_mixers_body, tm=tm, pos0=pos0, w=w),
        out_shape=[jax.ShapeDtypeStruct((b, t, 2 * w), BF16 if tm % 16 == 0 else F32),
                   jax.ShapeDtypeStruct((b, HALO_ROWS, 2 * w), F32)],
        grid=(b, t // tm),
        in_specs=[pl.BlockSpec((1, tm, d), lambda i, j: (i, j, 0)),
                  pl.BlockSpec((1, HALO_ROWS, 2 * w), lambda i, j: (i, 0, 0)),
                  _const_spec(wbd.shape), _const_spec(pscale.shape), _const_spec(cw.shape)],
        out_specs=[pl.BlockSpec((1, tm, 2 * w), lambda i, j: (i, j, 0)),
                   pl.BlockSpec((1, HALO_ROWS, 2 * w), lambda i, j: (i, 0, 0))],
        scratch_shapes=[pltpu.VMEM((HALO_ROWS + tm, 2 * w), F32)],
        compiler_params=_cparams("parallel", "arbitrary"),
        name="mixers",
    )(u_mix, halo, wbd, pscale, cw)


def _diff_lambda(lqk_ref, lam_init):
    lq = lqk_ref[...]
    a = jnp.sum(lq[0:1, :] * lq[1:2, :], axis=1, keepdims=True)
    b = jnp.sum(lq[2:3, :] * lq[3:4, :], axis=1, keepdims=True)
    return jnp.exp(a) - jnp.exp(b) + lam_init


def _lane_cumsum(x, steps):
    lane = lax.broadcasted_iota(jnp.int32, x.shape, 1)
    s = 1
    for _ in range(steps):
        x = x + jnp.where(lane >= s, pltpu.roll(x, s, 1), 0.0)
        s *= 2
    return x


def _flash_body(*refs, tq, tk, n_split, fox, lam_init):
    if fox:
        q_ref, k_ref, v_ref, frow_ref, fcol_ref, o_ref, vt_ref, fkc_ref, s0_ref, s1_ref, acc_ref = refs
    else:
        q_ref, k_ref, v_ref, lqk_ref, gcol_ref, o_ref, vt_ref, s0_ref, s1_ref, acc_ref = refs
    qi = pl.program_id(2)
    seq = k_ref.shape[0]
    half = HEAD_DIM
    chunk = min(seq, 512)

    @pl.when(qi == 0)
    def _():
        vt_ref[LANES:, :] = jnp.ones((vt_ref.shape[0] - LANES, seq), BF16)
        for c in range(seq // chunk):
            sl = slice(c * chunk, (c + 1) * chunk)
            vt_ref[0:LANES, sl] = v_ref[sl, :].astype(F32).T.astype(BF16)
            if fox:
                fc = fcol_ref[0, 0, sl, :] * LOG2E
                fkc_ref[0, sl, :] = jnp.broadcast_to(fc[:, 0:1], (chunk, LANES))
                fkc_ref[1, sl, :] = jnp.broadcast_to(fc[:, 1:2], (chunk, LANES))

    qt = q_ref[...].astype(F32).T
    low = lax.broadcasted_iota(jnp.int32, (LANES, tq), 0) < half
    qst = jnp.concatenate([jnp.where(low, qt, 0.0), jnp.where(low, 0.0, qt)], axis=1).astype(BF16)

    acc_ref[...] = jnp.zeros(acc_ref.shape, F32)
    q0 = pl.multiple_of(qi * tq, tq)
    cw = tq // n_split
    n_chain = 2 * n_split
    lanes = [slice(c * cw, (c + 1) * cw) for c in range(n_chain)]
    if fox:
        shift = [frow_ref[0, 0, c // n_split:c // n_split + 1, pl.ds(q0 + (c % n_split) * cw, cw)] * LOG2E
                 for c in range(n_chain)]
    n_rep = cw // LANES

    def chain_mode(c, tile):
        g = c % n_split
        return "full" if (tile is None or g > tile) else ("diag" if g == tile else "skip")

    def scores(ki, s_ref, tile=None):
        k = k_ref[pl.ds(pl.multiple_of(ki * tk, tk), tk), :]
        for c in range(n_chain):
            if chain_mode(c, tile) != "skip":
                s_ref[:, lanes[c]] = _dot(k, qst[:, lanes[c]])

    def softmax_pv(ki, s_ref, ms, tile=None):
        start = pl.multiple_of(ki * tk, tk)
        vt = vt_ref[:, pl.ds(start, tk)]
        new = []
        for c in range(n_chain):
            mode = chain_mode(c, tile)
            if mode == "skip":
                new.append(ms[c])
                continue
            st = s_ref[:, lanes[c]]
            if fox:
                st = st - _rep(fkc_ref[c // n_split, pl.ds(start, tk), :], n_rep)
            if mode == "diag":
                visible = (lax.broadcasted_iota(jnp.int32, (tk, cw), 0)
                           <= lax.broadcasted_iota(jnp.int32, (tk, cw), 1))
                st = jnp.where(visible, st, NEG_INF)
            m_cur = jnp.max(st, axis=0, keepdims=True)
            if fox:
                m_cur = m_cur + shift[c]
            m_new = jnp.maximum(ms[c], m_cur)
            alpha = jnp.exp2(ms[c] - m_new)
            p = jnp.exp2(st - ((m_new - shift[c]) if fox else m_new))
            acc_ref[:, lanes[c]] = alpha * acc_ref[:, lanes[c]] + _dot(vt, p.astype(BF16))
            new.append(m_new)
        return tuple(new)

    def two_tiles(j, ms, diagonal):
        scores(2 * j + 1, s1_ref, 1 if diagonal else None)
        ms = softmax_pv(2 * j, s0_ref, ms, 0 if diagonal else None)
        if not diagonal:
            scores(2 * j + 2, s0_ref)
        return softmax_pv(2 * j + 1, s1_ref, ms, 1 if diagonal else None)

    assert cw == tk and tq == 2 * tk
    scores(0, s0_ref)
    ms = lax.fori_loop(0, qi, lambda j, c: two_tiles(j, c, False), (jnp.full((1, cw), NEG_INF, F32),) * n_chain)
    two_tiles(qi, ms, True)

    acc = acc_ref[...]
    out = acc[0:LANES, :] / acc[LANES:LANES + 1, :]
    if fox:
        res = jnp.where(low, out[:, 0:tq], out[:, tq:])
    else:
        lam = _diff_lambda(lqk_ref, lam_init)
        o = out[:, 0:tq] - lam * out[:, tq:]
        ms = jnp.mean(o * o, axis=0, keepdims=True)
        res = o * lax.rsqrt(ms + RMS_EPS) * gcol_ref[...] * (1.0 - lam_init)
    o_ref[...] = res.T.astype(BF16)


def _flash(q, k, v, extra, *, batch, seq, tq, tk, fox, lam_init=0.0):
    m, w = q.shape
    n_blk = w // LANES
    n_q = seq // tq
    q_spec = pl.BlockSpec((tq, LANES), lambda b, h, i: (b * n_q + i, h))
    kv_spec = pl.BlockSpec((seq, LANES), lambda b, h, i: (b, h))
    assert tq == 2 * tk
    ones_rows = 16
    scratch = [pltpu.VMEM((LANES + ones_rows, seq), BF16)]
    if fox:
        extra_specs = [pl.BlockSpec((1, 1, 2, seq), lambda b, h, i: (b, h, 0, 0)),
                       pl.BlockSpec((1, 1, seq, 2), lambda b, h, i: (h, b, 0, 0))]
        scratch.append(pltpu.VMEM((2, seq, LANES), F32))
    else:
        extra_specs = [_const_spec(extra[0].shape), _const_spec(extra[1].shape)]
    scratch += [pltpu.VMEM((tk, 2 * tq), F32), pltpu.VMEM((tk, 2 * tq), F32),
                pltpu.VMEM((LANES + ones_rows, 2 * tq), F32)]
    return pl.pallas_call(
        functools.partial(_flash_body, tq=tq, tk=tk, n_split=max(1, tq // 256), fox=fox, lam_init=lam_init),
        out_shape=jax.ShapeDtypeStruct((m, w), BF16),
        grid=(batch, n_blk, n_q),
        in_specs=[q_spec, kv_spec, kv_spec] + extra_specs,
        out_specs=q_spec,
        scratch_shapes=scratch,
        compiler_params=_cparams("parallel", "parallel", "arbitrary"),
        name="flash_fox" if fox else "flash_diff",
    )(q, k, v, *extra)


def _fcum_prompt_body(lf_ref, o_ref, *, steps):
    o_ref[...] = _lane_cumsum(lf_ref[...], steps)


def _fcum_prompt(logft, *, batch, seq):
    spec = pl.BlockSpec((8, seq), lambda b: (0, b))
    return pl.pallas_call(
        functools.partial(_fcum_prompt_body, steps=(seq - 1).bit_length()),
        out_shape=jax.ShapeDtypeStruct(logft.shape, F32),
        grid=(batch,),
        in_specs=[spec],
        out_specs=spec,
        compiler_params=_cparams("parallel"),
        name="fcum_prompt",
    )(logft)


def _fcum_past_body(pt_ref, *refs, pages):
    in_refs = refs[:pages]
    o_ref, x_ref, carry_ref = refs[pages:]
    depth, _, n_heads, page = in_refs[0].shape
    grp = depth * n_heads
    g = pl.program_id(1)

    @pl.when(g == 0)
    def _():
        carry_ref[...] = jnp.zeros(carry_ref.shape, F32)

    for j in range(pages):
        for d in range(depth):
            x_ref[pl.ds(j * grp + d * n_heads, n_heads), :] = in_refs[j][d, 0]
    incl = _lane_cumsum(x_ref[...], int(math.log2(page)))
    tot = jnp.broadcast_to(incl[:, page - 1:page], incl.shape)
    row = lax.broadcasted_iota(jnp.int32, incl.shape, 0)
    pre = tot
    s = grp
    while s < pages * grp:
        pre = pre + jnp.where(row >= s, pltpu.roll(pre, s, 0), 0.0)
        s *= 2
    carry = carry_ref[...]
    x_ref[...] = incl + (pre - tot) + jnp.concatenate([carry] * pages, axis=0)
    carry_ref[...] = carry + pre[(pages - 1) * grp:, :]
    for j in range(pages):
        for d in range(depth):
            o_ref[d, 0, j] = x_ref[pl.ds(j * grp + d * n_heads, n_heads), :]


def _fcum_past(logf_t, page_table, *, pages):
    depth, n_pool, n_heads, page = logf_t.shape
    n_seq, n_pages = page_table.shape
    in_specs = [pl.BlockSpec((depth, 1, n_heads, page),
                             functools.partial(lambda b, g, pt, j: (0, pt[b, g * pages + j], 0, 0), j=j))
                for j in range(pages)]
    grid_spec = pltpu.PrefetchScalarGridSpec(
        num_scalar_prefetch=1,
        grid=(n_seq, n_pages // pages),
        in_specs=in_specs,
        out_specs=pl.BlockSpec((depth, 1, pages, n_heads, page), lambda b, g, pt: (0, b, g, 0, 0)),
        scratch_shapes=[pltpu.VMEM((pages * depth * n_heads, page), F32),
                        pltpu.VMEM((depth * n_heads, page), F32)])
    return pl.pallas_call(
        functools.partial(_fcum_past_body, pages=pages),
        out_shape=jax.ShapeDtypeStruct((depth, n_seq, n_pages, n_heads, page), F32),
        grid_spec=grid_spec,
        compiler_params=_cparams("parallel", "arbitrary"),
        name="fcum_past",
    )(page_table, *([logf_t] * pages))


def _softmax_step(s, pv, m_ref, l_ref, acc_ref, shift=None):
    n_k = s.shape[1] // LANES
    n_v = acc_ref.shape[1] // LANES
    m_prev = m_ref[...]
    m_cur = jnp.max(s, axis=1, keepdims=True)
    if shift is not None:
        m_cur = m_cur + shift
    m_new = jnp.maximum(m_prev, m_cur)
    alpha = jnp.exp(m_prev - m_new)
    sub = m_new if shift is None else m_new - shift
    p = jnp.exp(s - _rep(sub, n_k))
    l_ref[...] = alpha * l_ref[...] + jnp.sum(p, axis=1, keepdims=True)
    acc_ref[...] = _rep(alpha, n_v) * acc_ref[...] + pv(p.astype(BF16))
    m_ref[...] = m_new


def _pad_rows(x, rows):
    return jnp.concatenate([x, jnp.zeros((rows - x.shape[0], x.shape[1]), x.dtype)], axis=0)


def _decode_diff_body(pt_ref, q_ref, knew_ref, vnew_ref, lqk_ref, g_ref, *refs,
                      pages, n_heads, t_new, lam_init):
    k_refs = refs[:pages]
    v_refs = refs[pages:2 * pages]
    o_ref, qs_ref, m_ref, l_ref, acc_ref = refs[2 * pages:]
    g = pl.program_id(1)
    page = k_refs[0].shape[2] // n_heads
    hw = 2 * HEAD_DIM
    rows_h = 2 * t_new

    @pl.when(g == 0)
    def _():
        q = q_ref[0]
        low = lax.broadcasted_iota(jnp.int32, (t_new, hw), 1) < HEAD_DIM
        pieces = []
        for h in range(n_heads):
            qh = q[:, h * hw:(h + 1) * hw]
            pieces += [jnp.where(low, qh, 0.0), jnp.where(low, 0.0, qh)]
        qs_ref[...] = jnp.concatenate(pieces, axis=0).astype(BF16)
        m_ref[...] = jnp.full(m_ref.shape, NEG_INF, F32)
        l_ref[...] = jnp.zeros(l_ref.shape, F32)
        acc_ref[...] = jnp.zeros(acc_ref.shape, F32)

    def head_rows(page_refs, h):
        return jnp.concatenate([r[0, 0, pl.ds(h, page, stride=n_heads), :].astype(BF16) for r in page_refs],
                               axis=0)

    def per_head(fn):
        return jnp.concatenate([fn(h, slice(h * rows_h, (h + 1) * rows_h)) for h in range(n_heads)], axis=0)

    qs = qs_ref[...]
    vs = [head_rows(v_refs, h) for h in range(n_heads)]
    s = per_head(lambda h, rows: _dot_nt(qs[rows, :], head_rows(k_refs, h)))
    _softmax_step(s, lambda p: per_head(lambda h, rows: _dot(p[rows, :], vs[h])), m_ref, l_ref, acc_ref)

    @pl.when(g == pl.num_programs(1) - 1)
    def _():
        k_new = knew_ref[0]
        v_new = vnew_ref[0]
        kn = [_pad_rows(k_new[:, h * hw:(h + 1) * hw], LANES).astype(BF16) for h in range(n_heads)]
        vn = [_pad_rows(v_new[:, h * hw:(h + 1) * hw], LANES).astype(BF16) for h in range(n_heads)]
        s_new = per_head(lambda h, rows: _dot_nt(qs[rows, :], kn[h]))
        row = lax.broadcasted_iota(jnp.int32, s_new.shape, 0) & (t_new - 1)
        col = lax.broadcasted_iota(jnp.int32, s_new.shape, 1)
        s_new = jnp.where(col <= row, s_new, NEG_INF)
        _softmax_step(s_new, lambda p: per_head(lambda h, rows: _dot(p[rows, :], vn[h])), m_ref, l_ref, acc_ref)

        out = acc_ref[...] / l_ref[...]
        lam = _diff_lambda(lqk_ref, lam_init)
        for h in range(n_heads):
            o = out[h * rows_h:h * rows_h + t_new, :] - lam * out[h * rows_h + t_new:(h + 1) * rows_h, :]
            ms = jnp.mean(o * o, axis=-1, keepdims=True)
            o_ref[0, :, h * hw:(h + 1) * hw] = o * lax.rsqrt(ms + RMS_EPS) * g_ref[...] * (1.0 - lam_init)


def _decode_diff(q, k_new, v_new, cache_k, cache_v, page_table, lqk, g, *, layer, pages, lam_init):
    n_seq, t_new, width = q.shape
    n_heads = width // (2 * HEAD_DIM)
    n_pages = page_table.shape[1]
    rows = cache_k.shape[2]
    seq_spec = pl.BlockSpec((1, t_new, width), lambda b, g_, pt: (b, 0, 0))
    page_specs = [pl.BlockSpec((1, 1, rows, LANES),
                               functools.partial(lambda b, g_, pt, j: (layer, pt[b, g_ * pages + j], 0, 0), j=j))
                  for j in range(pages)]
    n_rows = n_heads * 2 * t_new
    grid_spec = pltpu.PrefetchScalarGridSpec(
        num_scalar_prefetch=1,
        grid=(n_seq, n_pages // pages),
        in_specs=[seq_spec, seq_spec, seq_spec,
                  pl.BlockSpec(lqk.shape, lambda b, g_, pt: (0, 0)),
                  pl.BlockSpec(g.shape, lambda b, g_, pt: (0, 0))] + page_specs + page_specs,
        out_specs=seq_spec,
        scratch_shapes=[pltpu.VMEM((n_rows, LANES), BF16), pltpu.VMEM((n_rows, LANES), F32),
                        pltpu.VMEM((n_rows, LANES), F32), pltpu.VMEM((n_rows, LANES), F32)])
    return pl.pallas_call(
        functools.partial(_decode_diff_body, pages=pages, n_heads=n_heads, t_new=t_new, lam_init=lam_init),
        out_shape=jax.ShapeDtypeStruct((n_seq, t_new, width), F32),
        grid_spec=grid_spec,
        compiler_params=_cparams("parallel", "arbitrary"),
        name="decode_diff",
    )(page_table, q, k_new, v_new, lqk, g, *([cache_k] * pages), *([cache_v] * pages))


def _decode_fox_body(pt_ref, q_ref, knew_ref, vnew_ref, lfnew_ref, fk_ref, fklast_ref, *refs,
                     pages, n_heads, t_new):
    kt_refs = refs[:pages]
    vt_refs = refs[pages:2 * pages]
    o_ref, qs_ref, m_ref, l_ref, acc_ref = refs[2 * pages:]
    g = pl.program_id(1)
    width = q_ref.shape[2]
    page = kt_refs[0].shape[3]
    n_rows = n_heads * t_new

    @pl.when(g == 0)
    def _():
        q = q_ref[0]
        lane = lax.broadcasted_iota(jnp.int32, (t_new, width), 1)
        qs_ref[...] = jnp.concatenate(
            [jnp.where((lane >= h * HEAD_DIM) & (lane < (h + 1) * HEAD_DIM), q, 0.0) for h in range(n_heads)],
            axis=0).astype(BF16)
        m_ref[...] = jnp.full(m_ref.shape, NEG_INF, F32)
        l_ref[...] = jnp.zeros(l_ref.shape, F32)
        acc_ref[...] = jnp.zeros(acc_ref.shape, F32)

    def head_rows(per_head):
        return jnp.concatenate([jnp.broadcast_to(per_head(h), (t_new, per_head(h).shape[1]))
                                for h in range(n_heads)], axis=0)

    total = fklast_ref[0, 0, 0, :, page - 1:page]
    f_new = _lane_cumsum(lfnew_ref[0], int(math.log2(t_new))) + jnp.concatenate(
        [total, jnp.zeros((8 - n_heads, 1), F32)], axis=0)
    eye = (lax.broadcasted_iota(jnp.int32, (t_new, LANES), 0)
           == lax.broadcasted_iota(jnp.int32, (t_new, LANES), 1))
    fq_col = jnp.concatenate(
        [jnp.sum(jnp.where(eye, jnp.broadcast_to(f_new[h:h + 1, :], (t_new, LANES)), 0.0),
                 axis=1, keepdims=True) for h in range(n_heads)], axis=0)
    shift = jnp.broadcast_to(fq_col, (n_rows, LANES))

    qs = qs_ref[...]
    kt = jnp.concatenate([r[0, 0].astype(BF16) for r in kt_refs], axis=1)
    vt = jnp.concatenate([r[0, 0].astype(BF16) for r in vt_refs], axis=1)
    s = _dot(qs, kt)
    s = s - jnp.concatenate([head_rows(lambda h: fk_ref[0, 0, j, h:h + 1, :]) for j in range(pages)], axis=1)
    _softmax_step(s, lambda p: _dot_nt(p, vt), m_ref, l_ref, acc_ref, shift)

    @pl.when(g == pl.num_programs(1) - 1)
    def _():
        k_new = _pad_rows(knew_ref[0], LANES).astype(BF16)
        v_new = _pad_rows(vnew_ref[0], LANES).astype(BF16)
        s_new = _dot_nt(qs, k_new) - head_rows(lambda h: f_new[h:h + 1, :])
        row = lax.broadcasted_iota(jnp.int32, s_new.shape, 0) & (t_new - 1)
        col = lax.broadcasted_iota(jnp.int32, s_new.shape, 1)
        s_new = jnp.where(col <= row, s_new, NEG_INF)
        _softmax_step(s_new, lambda p: _dot(p, v_new), m_ref, l_ref, acc_ref, shift)

        out = acc_ref[...] / _rep(l_ref[...], width // LANES)
        lane = lax.broadcasted_iota(jnp.int32, (t_new, width), 1)
        res = jnp.zeros((t_new, width), F32)
        for h in range(n_heads):
            keep = (lane >= h * HEAD_DIM) & (lane < (h + 1) * HEAD_DIM)
            res = jnp.where(keep, out[h * t_new:(h + 1) * t_new, :], res)
        o_ref[0] = res


def _decode_fox(q, k_new, v_new, cache_kt, cache_vt, page_table, lf_new, fk, *, layer, pages):
    n_seq, t_new, width = q.shape
    n_heads = width // HEAD_DIM
    page = cache_kt.shape[3]
    n_pages = page_table.shape[1]
    seq_spec = pl.BlockSpec((1, t_new, width), lambda b, g, pt: (b, 0, 0))
    page_specs = [pl.BlockSpec((1, 1, width, page),
                               functools.partial(lambda b, g, pt, j: (layer, pt[b, g * pages + j], 0, 0), j=j))
                  for j in range(pages)]
    n_rows = n_heads * t_new
    grid_spec = pltpu.PrefetchScalarGridSpec(
        num_scalar_prefetch=1,
        grid=(n_seq, n_pages // pages),
        in_specs=[seq_spec, seq_spec, seq_spec,
                  pl.BlockSpec((1, 8, LANES), lambda b, g, pt: (b, 0, 0)),
                  pl.BlockSpec((1, 1, pages, n_heads, page), lambda b, g, pt: (layer, b, g, 0, 0)),
                  pl.BlockSpec((1, 1, 1, n_heads, page), lambda b, g, pt: (layer, b, n_pages - 1, 0, 0))]
        + page_specs + page_specs,
        out_specs=seq_spec,
        scratch_shapes=[pltpu.VMEM((n_rows, width), BF16), pltpu.VMEM((n_rows, LANES), F32),
                        pltpu.VMEM((n_rows, LANES), F32), pltpu.VMEM((n_rows, width), F32)])
    return pl.pallas_call(
        functools.partial(_decode_fox_body, pages=pages, n_heads=n_heads, t_new=t_new),
        out_shape=jax.ShapeDtypeStruct((n_seq, t_new, width), F32),
        grid_spec=grid_spec,
        compiler_params=_cparams("parallel", "arbitrary"),
        name="decode_fox",
    )(page_table, q, k_new, v_new, lf_new, fk, fk, *([cache_kt] * pages), *([cache_vt] * pages))


def _merge_body(x_ref, yad_ref, yb_ref, yc_ref, wg_ref, wp_ref, wo_ref, g_ref, b_ref, o_ref, *, alpha):
    x = x_ref[...]
    xb = x.astype(BF16)
    d = x.shape[1]
    w = d // 4
    yad = yad_ref[...]
    ys = (yad[:, 0:w], yb_ref[...], yc_ref[...], yad[:, w:2 * w])
    merged = None
    row = 0
    for i, y in enumerate(ys):
        gate = jax.nn.sigmoid(_dot(xb, wg_ref[:, i * d:(i + 1) * d]))
        term = gate * _dot(y.astype(BF16), wp_ref[row:row + y.shape[1], :])
        merged = term if merged is None else merged + term
        row += y.shape[1]
    out = _dot(merged.astype(BF16), wo_ref[...])
    o_ref[...] = _layer_norm(alpha * x + out, g_ref[...], b_ref[...])


def _merge(x, yad, yb, yc, wg, wp, wo, g, b, *, tm, alpha):
    m, d = x.shape
    row = lambda w: pl.BlockSpec((tm, w), lambda i: (i, 0))
    return pl.pallas_call(
        functools.partial(_merge_body, alpha=alpha),
        out_shape=jax.ShapeDtypeStruct((m, d), F32),
        grid=(m // tm,),
        in_specs=[row(d), row(yad.shape[1]), row(yb.shape[1]), row(yc.shape[1]),
                  _const_spec(wg.shape), _const_spec(wp.shape), _const_spec(wo.shape),
                  _const_spec((1, d)), _const_spec((1, d))],
        out_specs=row(d),
        compiler_params=_cparams("parallel"),
        name="merge",
    )(x, yad, yb, yc, wg, wp, wo, g, b)


def _rope_tables(pos):
    half = HEAD_DIM // 2
    inv = ROPE_THETA ** (-jnp.arange(half, dtype=F32) / half)
    ang = pos.astype(F32)[:, None] * inv[None, :]
    cos, sin = jnp.cos(ang), jnp.sin(ang)
    cos_t = jnp.tile(cos, (1, LANES // half))
    sin_t = jnp.tile(jnp.concatenate([-sin, sin], axis=1), (1, LANES // HEAD_DIM))
    return cos_t, sin_t


def _pick_tile(n, target):
    t = min(n, target)
    while n % t:
        t //= 2
    return t


def kernel(x_prompt, x_sample, cache_diff_k, cache_diff_v, cache_fox_k, cache_fox_v, cache_fox_logf,
           state_pool, state_conv, page_table, w_in, b_fgate, pool_w, pool_scale, lambda_qk, subln_g,
           conv_w, w_branch, w_o, w_ffn1_gate, w_ffn1_up, w_ffn1_down, w_ffn2_gate, w_ffn2_up,
           w_ffn2_down, ln_g, ln_b):
    batch, seq, d = x_prompt.shape
    n_seq, t_new, _ = x_sample.shape
    depth = w_in.shape[0]
    n_pool, page = cache_diff_k.shape[1:3]
    past_len = page_table.shape[1] * page
    h_diff = cache_diff_k.shape[3]
    h_fox = cache_fox_k.shape[3]
    w_pool = state_pool.shape[-1]
    w_diff, w_fox, w_conv = d // 2, d // 4, d // 4
    alpha = (2.0 * depth) ** 0.25

    sizes = (w_pool, w_diff, w_diff, w_diff, w_fox, w_fox, w_fox, h_fox, w_conv, w_conv, w_conv, 4 * d)
    offs = np.concatenate([[0], np.cumsum(sizes)])
    col = lambda i: w_in[:, :, offs[i]:offs[i + 1]]
    w_main = jnp.concatenate([col(0), col(8), col(9), col(10), col(1), col(2), col(3), col(4), col(5), col(6)],
                             axis=-1).astype(BF16)
    w_ft = jnp.pad(jnp.swapaxes(col(7), 1, 2), ((0, 0), (0, 16 - h_fox), (0, 0))).astype(BF16)
    b_f = jnp.pad(b_fgate, ((0, 0), (0, 8 - h_fox)))[:, :, None]
    w_gate = col(11).astype(BF16)
    w_br = w_branch.astype(BF16)
    w_out = w_o.astype(BF16)
    ffn = [tuple(w.astype(BF16) for w in ws) for ws in
           ((w_ffn1_gate, w_ffn1_up, w_ffn1_down), (w_ffn2_gate, w_ffn2_up, w_ffn2_down))]
    n_grp, grp = pool_w.shape[1], pool_w.shape[2]
    wbd = jnp.zeros((depth, w_pool, w_pool), F32)
    for gidx in range(n_grp):
        wbd = wbd.at[:, gidx * grp:(gidx + 1) * grp, gidx * grp:(gidx + 1) * grp].set(pool_w[:, gidx])
    wbd = wbd.astype(BF16)

    cos_p, sin_p = _rope_tables(jnp.arange(seq, dtype=jnp.int32))
    cos_s, sin_s = _rope_tables(past_len + jnp.arange(t_new, dtype=jnp.int32))
    cos_s, sin_s = jnp.tile(cos_s, (n_seq, 1)), jnp.tile(sin_s, (n_seq, 1))

    ck_diff = cache_diff_k.reshape(depth, n_pool, page * h_diff, 2 * HEAD_DIM)
    cv_diff = cache_diff_v.reshape(depth, n_pool, page * h_diff, 2 * HEAD_DIM)
    ck_fox = jnp.transpose(cache_fox_k, (0, 1, 3, 4, 2)).reshape(depth, n_pool, h_fox * HEAD_DIM, page)
    cv_fox = jnp.transpose(cache_fox_v, (0, 1, 3, 4, 2)).reshape(depth, n_pool, h_fox * HEAD_DIM, page)
    pages = _pick_tile(page_table.shape[1], 16)
    fk_past = _fcum_past(jnp.transpose(cache_fox_logf, (0, 1, 3, 2)), page_table, pages=pages)

    halo_p = jnp.zeros((batch, HALO_ROWS, 2 * w_pool), F32)

    tm_p = _pick_tile(batch * seq, 512)
    tm_s = n_seq * t_new
    tq = _pick_tile(seq, 512)
    tk = tq // 2

    pages_fox = _pick_tile(page_table.shape[1], 32)
    p_kv = [jnp.zeros((depth, batch * seq * h_diff, 2 * HEAD_DIM), F32) for _ in range(2)]
    p_kv += [jnp.zeros((depth, batch, h_fox * HEAD_DIM, seq), F32) for _ in range(2)]

    xp = x_prompt.reshape(batch * seq, d)
    xs = x_sample.reshape(n_seq * t_new, d)
    p_rows, s_rows = [], []
    for l in range(depth):
        lam_init = 0.8 - 0.6 * math.exp(-0.3 * l)
        lng = lambda i: ln_g[l, i][None, :]
        lnb = lambda i: ln_b[l, i][None, :]
        halo_s = jnp.concatenate(
            [jnp.pad(state_pool[l], ((0, 0), (HALO_ROWS - state_pool.shape[2], 0), (0, 0))),
             jnp.pad(state_conv[l], ((0, 0), (HALO_ROWS - state_conv.shape[2], 0), (0, 0)))], axis=-1)

        def mixing(x, n_b, t, tm, cos_t, sin_t, halo, pos0, q_scale, stacked=None):
            outs = _in_proj(x, w_main[l], w_ft[l], b_f[l], cos_t, sin_t, tm=tm, q_scale=q_scale,
                            stacked=stacked)
            yad, state = _mixers(outs[0].reshape(n_b, t, d), halo, wbd[l], pool_scale[l][None, :], conv_w[l],
                                 tm=min(t, 512), pos0=pos0)
            return outs[1:], yad.reshape(n_b * t, 2 * w_pool), state

        xp = _ffn_ln(xp, *[w[l] for w in ffn[0]], lng(0), lnb(0), tm=tm_p, alpha=alpha)
        (qd, kd16, vd16, qf, kf16, vf16, logft, *p_kv), yad, state = mixing(
            xp, batch, seq, tm_p, cos_p, sin_p, halo_p, 0, LOG2E / math.sqrt(HEAD_DIM),
            stacked=(l, *p_kv))
        yb = _flash(qd, kd16, vd16, (lambda_qk[l], subln_g[l][:, None]), batch=batch, seq=seq, tq=tq, tk=tk,
                    fox=False, lam_init=lam_init)
        logf = logft[:h_fox].reshape(h_fox, batch, seq)
        f_cum = _fcum_prompt(logft, batch=batch, seq=seq)[:h_fox].reshape(h_fox // 2, 2, batch, seq)
        f_row = jnp.transpose(f_cum, (2, 0, 1, 3))
        f_col = jnp.transpose(f_cum, (0, 2, 3, 1))
        yc = _flash(qf, kf16, vf16, (f_row, f_col), batch=batch, seq=seq, tq=tq, tk=tk, fox=True)
        xp = _merge(xp, yad, yb, yc, w_gate[l], w_br[l], w_out[l], lng(1), lnb(1), tm=tm_p, alpha=alpha)
        xp = _ffn_ln(xp, *[w[l] for w in ffn[1]], lng(2), lnb(2), tm=tm_p, alpha=alpha)
        p_rows.append((jnp.transpose(logf, (1, 2, 0)),
                       state[:, 1:, :w_pool], state[:, HALO_ROWS - state_conv.shape[2]:, w_pool:]))

        xs = _ffn_ln(xs, *[w[l] for w in ffn[0]], lng(0), lnb(0), tm=tm_s, alpha=alpha)
        (qd, _, _, qf, _, _, logft, kd32, vd32, kf32, vf32), yad, state = mixing(
            xs, n_seq, t_new, tm_s, cos_s, sin_s, halo_s, past_len, 1.0 / math.sqrt(HEAD_DIM))
        seq3 = lambda a: a.reshape(n_seq, t_new, a.shape[-1])
        yb = _decode_diff(seq3(qd.astype(F32)), seq3(kd32), seq3(vd32), ck_diff, cv_diff, page_table,
                          lambda_qk[l], subln_g[l][None, :], layer=l, pages=pages, lam_init=lam_init)
        logf = logft[:h_fox].reshape(h_fox, n_seq, t_new)
        lf_new = jnp.pad(jnp.transpose(logf, (1, 0, 2)), ((0, 0), (0, 8 - h_fox), (0, LANES - t_new)))
        yc = _decode_fox(seq3(qf.astype(F32)), seq3(kf32), seq3(vf32), ck_fox, cv_fox, page_table,
                         lf_new, fk_past, layer=l, pages=pages_fox)
        xs = _merge(xs, yad, yb.reshape(n_seq * t_new, -1), yc.reshape(n_seq * t_new, -1),
                    w_gate[l], w_br[l], w_out[l], lng(1), lnb(1), tm=tm_s, alpha=alpha)
        xs = _ffn_ln(xs, *[w[l] for w in ffn[1]], lng(2), lnb(2), tm=tm_s, alpha=alpha)
        s_rows.append((kd32.reshape(n_seq, t_new, h_diff, -1), vd32.reshape(n_seq, t_new, h_diff, -1),
                       kf32.reshape(n_seq, t_new, h_fox, -1), vf32.reshape(n_seq, t_new, h_fox, -1),
                       jnp.transpose(logf, (1, 2, 0)),
                       state[:, 1:, :w_pool], state[:, HALO_ROWS - state_conv.shape[2]:, w_pool:]))

    p_small = [jnp.stack([r[i] for r in p_rows], axis=0) for i in range(3)]
    s_out = [jnp.stack([r[i] for r in s_rows], axis=0) for i in range(7)]
    kd_all, vd_all, kf_all, vf_all = p_kv
    fox_rows = lambda a: jnp.transpose(a.reshape(depth, batch, h_fox, HEAD_DIM, seq), (0, 1, 4, 2, 3))
    return (xp.reshape(batch, seq, d), xs.reshape(n_seq, t_new, d),
            kd_all.reshape(depth, batch, seq, h_diff, 2 * HEAD_DIM),
            vd_all.reshape(depth, batch, seq, h_diff, 2 * HEAD_DIM),
            fox_rows(kf_all), fox_rows(vf_all), *p_small, *s_out)
```

```python
import functools
import math

import jax
import jax.numpy as jnp
import numpy as np
from jax import lax
from jax.experimental import pallas as pl
from jax.experimental.pallas import tpu as pltpu

F32 = jnp.float32
BF16 = jnp.bfloat16

HEAD_DIM = 64
ROPE_THETA = 10000.0
LN_EPS = 1e-5
RMS_EPS = 1e-5
NEG_INF = -1e30
LOG2E = math.log2(math.e)
POOL_WINDOWS = (2, 4, 8, 16)
HALO_ROWS = 16
LANES = 128
VMEM_LIMIT_BYTES = 56 * 1024 * 1024


def _cparams(*sem):
    return pltpu.CompilerParams(dimension_semantics=sem, vmem_limit_bytes=VMEM_LIMIT_BYTES)


def _const_spec(shape):
    zeros = (0,) * len(shape)
    return pl.BlockSpec(shape, lambda *_: zeros, pipeline_mode=pl.Buffered(1))


def _layer_spec(arr, layer):
    zeros = (0,) * (arr.ndim - 1)
    return pl.BlockSpec((None,) + arr.shape[1:], lambda *_: (layer,) + zeros, pipeline_mode=pl.Buffered(1))


def _dot(a, b):
    return jnp.dot(a, b, preferred_element_type=F32)


def _dot_nt(a, b):
    return lax.dot_general(a, b, (((1,), (1,)), ((), ())), preferred_element_type=F32)


def _rep(x, n):
    return x if n == 1 else jnp.concatenate([x] * n, axis=1)


def _layer_norm(z, g, b):
    mu = jnp.mean(z, axis=-1, keepdims=True)
    zc = z - mu
    var = jnp.mean(zc * zc, axis=-1, keepdims=True)
    return zc * lax.rsqrt(var + LN_EPS) * g + b


def _ffn_ln_body(x_ref, wg_ref, wu_ref, wd_ref, g_ref, b_ref, o_ref, acc_ref, *, chunk, alpha):
    x = x_ref[...]
    xb = x.astype(BF16)
    d_ff = wg_ref.shape[1]
    for c in range(d_ff // chunk):
        sl = slice(c * chunk, (c + 1) * chunk)
        gate = _dot(xb, wg_ref[:, sl])
        up = _dot(xb, wu_ref[:, sl])
        h = gate * jax.nn.sigmoid(gate) * up
        d = _dot(h.astype(BF16), wd_ref[sl, :])
        if c == 0:
            acc_ref[...] = d
        else:
            acc_ref[...] += d
    z = alpha * x + 0.5 * acc_ref[...]
    o_ref[...] = _layer_norm(z, g_ref[...], b_ref[...])


def _ffn_ln(x, wg, wu, wd, g, b, *, layer, tm, alpha):
    m, d = x.shape
    row = pl.BlockSpec((tm, d), lambda i: (i, 0))
    return pl.pallas_call(
        functools.partial(_ffn_ln_body, chunk=256, alpha=alpha),
        out_shape=jax.ShapeDtypeStruct((m, d), F32),
        grid=(m // tm,),
        in_specs=[row, _layer_spec(wg, layer), _layer_spec(wu, layer), _layer_spec(wd, layer),
                  _const_spec((1, d)), _const_spec((1, d))],
        out_specs=row,
        scratch_shapes=[pltpu.VMEM((tm, d), F32)],
        compiler_params=_cparams("parallel"),
        name="ffn_ln",
    )(x, wg, wu, wd, g, b)


def _mix_tile(u, ext_ref, wbd, pscale, cw, pos_start):
    tm = u.shape[0]
    w = u.shape[1] // 4
    up = u[:, 0:w]
    cb = u[:, w:2 * w]
    ext_ref[HALO_ROWS:HALO_ROWS + tm, 0:w] = up
    ext_ref[HALO_ROWS:HALO_ROWS + tm, w:2 * w] = u[:, 2 * w:3 * w] * u[:, 3 * w:4 * w]
    e = ext_ref[...]
    ep = e[:, 0:w]
    ev = e[:, w:2 * w]

    sums = []
    b = ep
    for k in (1, 2, 4, 8):
        b = b + pltpu.roll(b, k, 0)
        sums.append(b[HALO_ROWS:, :])
    lane = lax.broadcasted_iota(jnp.int32, (tm, w), 1)
    grp = w // len(POOL_WINDOWS)
    win_sum = jnp.where(lane < grp, sums[0],
                        jnp.where(lane < 2 * grp, sums[1], jnp.where(lane < 3 * grp, sums[2], sums[3])))
    window = jnp.where(lane < grp, POOL_WINDOWS[0],
                       jnp.where(lane < 2 * grp, POOL_WINDOWS[1],
                                 jnp.where(lane < 3 * grp, POOL_WINDOWS[2], POOL_WINDOWS[3])))
    pos = pos_start + lax.broadcasted_iota(jnp.int32, (tm, w), 0)
    cnt = jnp.minimum(pos + 1, window).astype(F32)
    pooled = win_sum / cnt - up
    ya = _dot(pooled.astype(BF16), wbd) * pscale

    conv = cw[0:1, :] * pltpu.roll(ev, 2, 0) + cw[1:2, :] * pltpu.roll(ev, 1, 0) + cw[2:3, :] * ev
    yd = cb * conv[HALO_ROWS:, :]
    return ya, yd, e[tm:tm + HALO_ROWS, :]


def _in_proj_body(*refs, d, q_scale, stacked, n_tab):
    if stacked:
        (x_ref, w_ref, wft_ref, bf_ref, cos_ref, sin_ref, halo_ref, wbd_ref, pscale_ref, cw_ref,
         _, _, _, _,
         yad_ref, state_ref, qd_ref, kd16_ref, vd16_ref, qf_ref, kf16_ref, vf16_ref, logft_ref,
         kd32_ref, vd32_ref, kf32_ref, vf32_ref, ext_ref) = refs
    else:
        (x_ref, w_ref, wft_ref, bf_ref, cos_ref, sin_ref,
         umix_ref, qd_ref, kd16_ref, vd16_ref, qf_ref, kf16_ref, vf16_ref, logft_ref,
         kd32_ref, vd32_ref, kf32_ref, vf32_ref) = refs
    w_mix, w_diff, w_fox = d, d // 2, d // 4
    n_blk = w_diff // LANES
    xb = x_ref[...].astype(BF16)
    tm = xb.shape[0]

    def put_diff(ref, j, val):
        if stacked:
            ref[0, pl.ds(j, tm, stride=n_blk), :] = val
        else:
            ref[:, j * LANES:(j + 1) * LANES] = val

    def put_fox(ref, val):
        if stacked:
            ref[0, 0] = val.T
        else:
            ref[...] = val

    u_mix = _dot(xb, w_ref[:, 0:w_mix])
    if stacked:
        ti = pl.program_id(0) % n_tab

        @pl.when(ti == 0)
        def _():
            ext_ref[0:HALO_ROWS, :] = halo_ref[0]

        ya, yd, new_halo = _mix_tile(u_mix, ext_ref, wbd_ref[...], pscale_ref[...], cw_ref[...], ti * tm)
        yad_ref[:, 0:w_fox] = ya.astype(BF16)
        yad_ref[:, w_fox:2 * w_fox] = yd.astype(BF16)
        ext_ref[0:HALO_ROWS, :] = new_halo
        state_ref[0] = new_halo
    else:
        umix_ref[...] = u_mix

    qk = _dot(xb, w_ref[:, w_mix:w_mix + 2 * w_diff])
    cos = cos_ref[...]
    sin = sin_ref[...]
    lane = lax.broadcasted_iota(jnp.int32, (tm, LANES), 1)
    first_half = (lane & (HEAD_DIM // 2)) == 0
    for j in range(2 * n_blk):
        blk = qk[:, j * LANES:(j + 1) * LANES]
        partner = jnp.where(first_half, pltpu.roll(blk, LANES - HEAD_DIM // 2, 1),
                            pltpu.roll(blk, HEAD_DIM // 2, 1))
        r = blk * cos + partner * sin
        if j < n_blk:
            qd_ref[:, j * LANES:(j + 1) * LANES] = (r * q_scale).astype(BF16)
        else:
            put_diff(kd32_ref, j - n_blk, r)
            kd16_ref[:, (j - n_blk) * LANES:(j - n_blk + 1) * LANES] = r.astype(BF16)

    off = w_mix + 2 * w_diff
    vd = _dot(xb, w_ref[:, off:off + w_diff])
    for j in range(n_blk):
        put_diff(vd32_ref, j, vd[:, j * LANES:(j + 1) * LANES])
    vd16_ref[...] = vd.astype(BF16)

    off += w_diff
    fox = _dot(xb, w_ref[:, off:off + 3 * w_fox])
    qf_ref[...] = (fox[:, 0:w_fox] * q_scale).astype(BF16)
    kf = fox[:, w_fox:2 * w_fox]
    put_fox(kf32_ref, kf)
    kf16_ref[...] = kf.astype(BF16)
    vf = fox[:, 2 * w_fox:3 * w_fox]
    put_fox(vf32_ref, vf)
    vf16_ref[...] = vf.astype(BF16)

    ft = _dot_nt(wft_ref[...], xb)
    logft_ref[...] = jax.nn.log_sigmoid(ft[0:8, :] + bf_ref[...])


def _in_proj(x, w_main, w_ft, b_f, cos_t, sin_t, *, layer, tm, q_scale, stacked=None):
    m, d = x.shape
    n_tab = cos_t.shape[0] // tm
    w_diff, w_fox = d // 2, d // 4
    n_blk = w_diff // LANES
    row = lambda w: pl.BlockSpec((tm, w), lambda i: (i, 0))
    tab = pl.BlockSpec((tm, LANES), lambda i: (i % n_tab, 0))
    sds = jax.ShapeDtypeStruct
    out_shape = [sds((m, w_diff), BF16), sds((m, w_diff), BF16), sds((m, w_diff), BF16),
                 sds((m, w_fox), BF16), sds((m, w_fox), BF16), sds((m, w_fox), BF16), sds((8, m), F32)]
    out_specs = [row(w_diff), row(w_diff), row(w_diff), row(w_fox), row(w_fox), row(w_fox),
                 pl.BlockSpec((8, tm), lambda i: (0, i))]
    in_specs = [row(d), _layer_spec(w_main, layer), _layer_spec(w_ft, layer), _layer_spec(b_f, layer), tab, tab]
    args = [x, w_main, w_ft, b_f, cos_t, sin_t]
    aliases = {}
    scratch = []
    if stacked is None:
        out_shape = [sds((m, d), F32)] + out_shape + [sds((m, w_diff), F32), sds((m, w_diff), F32),
                                                      sds((m, w_fox), F32), sds((m, w_fox), F32)]
        out_specs = [row(d)] + out_specs + [row(w_diff), row(w_diff), row(w_fox), row(w_fox)]
    else:
        halo, wbd, pscale, cw = stacked[:4]
        bufs = list(stacked[4:])
        seq_block = lambda i: (i // n_tab, 0, 0)
        in_specs += [pl.BlockSpec((1,) + halo.shape[1:], seq_block), _layer_spec(wbd, layer),
                     _layer_spec(pscale, layer), _layer_spec(cw, layer)]
        args += [halo, wbd, pscale, cw]
        out_shape = [sds((m, 2 * w_fox), BF16), sds(halo.shape, F32)] + out_shape
        out_specs = [row(2 * w_fox), pl.BlockSpec((1,) + halo.shape[1:], seq_block)] + out_specs
        diff_spec = pl.BlockSpec((1, tm * n_blk, LANES), lambda i: (layer, i, 0))
        fox_spec = pl.BlockSpec((1, 1, w_fox, tm), lambda i: (layer, i // n_tab, 0, i % n_tab))
        aliases = {len(args) + k: len(out_shape) + k for k in range(len(bufs))}
        out_shape += [sds(b.shape, b.dtype) for b in bufs]
        out_specs += [diff_spec, diff_spec, fox_spec, fox_spec]
        in_specs += [pl.BlockSpec(memory_space=pl.ANY)] * len(bufs)
        args += bufs
        scratch = [pltpu.VMEM((HALO_ROWS + tm, 2 * w_fox), F32)]
    return pl.pallas_call(
        functools.partial(_in_proj_body, d=d, q_scale=q_scale, stacked=stacked is not None, n_tab=n_tab),
        out_shape=out_shape,
        grid=(m // tm,),
        in_specs=in_specs,
        out_specs=out_specs,
        scratch_shapes=scratch,
        input_output_aliases=aliases,
        compiler_params=_cparams("arbitrary"),
        name="in_proj",
    )(*args)


def _mixers_body(u_ref, halo_ref, wbd_ref, pscale_ref, cw_ref, yad_ref, state_ref, ext_ref,
                 *, tm, pos0, w):
    out_dtype = yad_ref.dtype
    ti = pl.program_id(1)

    @pl.when(ti == 0)
    def _():
        ext_ref[0:HALO_ROWS, :] = halo_ref[0]

    ya, yd, new_halo = _mix_tile(u_ref[0], ext_ref, wbd_ref[...], pscale_ref[...], cw_ref[...], pos0 + ti * tm)
    yad_ref[0, :, 0:w] = ya.astype(out_dtype)
    yad_ref[0, :, w:2 * w] = yd.astype(out_dtype)
    ext_ref[0:HALO_ROWS, :] = new_halo
    state_ref[0] = new_halo


def _mixers(u_mix, halo, wbd, pscale, cw, *, layer, tm, pos0):
    b, t, d = u_mix.shape
    w = d // 4
    return pl.pallas_call(
        functools.partial(_mixers_body, tm=tm, pos0=pos0, w=w),
        out_shape=[jax.ShapeDtypeStruct((b, t, 2 * w), BF16 if tm % 16 == 0 else F32),
                   jax.ShapeDtypeStruct((b, HALO_ROWS, 2 * w), F32)],
        grid=(b, t // tm),
        in_specs=[pl.BlockSpec((1, tm, d), lambda i, j: (i, j, 0)),
                  pl.BlockSpec((1, HALO_ROWS, 2 * w), lambda i, j: (i, 0, 0)),
                  _layer_spec(wbd, layer), _layer_spec(pscale, layer), _layer_spec(cw, layer)],
        out_specs=[pl.BlockSpec((1, tm, 2 * w), lambda i, j: (i, j, 0)),
                   pl.BlockSpec((1, HALO_ROWS, 2 * w), lambda i, j: (i, 0, 0))],
        scratch_shapes=[pltpu.VMEM((HALO_ROWS + tm, 2 * w), F32)],
        compiler_params=_cparams("parallel", "arbitrary"),
        name="mixers",
    )(u_mix, halo, wbd, pscale, cw)


def _diff_lambda(lqk_ref, lam_init):
    lq = lqk_ref[...]
    a = jnp.sum(lq[0:1, :] * lq[1:2, :], axis=1, keepdims=True)
    b = jnp.sum(lq[2:3, :] * lq[3:4, :], axis=1, keepdims=True)
    return jnp.exp(a) - jnp.exp(b) + lam_init


def _lane_cumsum(x, steps):
    lane = lax.broadcasted_iota(jnp.int32, x.shape, 1)
    s = 1
    for _ in range(steps):
        x = x + jnp.where(lane >= s, pltpu.roll(x, s, 1), 0.0)
        s *= 2
    return x


def _flash_body(*refs, tq, tk, n_split, fox, lam_init):
    if fox:
        q_ref, k_ref, v_ref, frow_ref, fcol_ref, o_ref, vt_ref, fkc_ref, s0_ref, s1_ref, acc_ref = refs
    else:
        q_ref, k_ref, v_ref, lqk_ref, gcol_ref, o_ref, vt_ref, s0_ref, s1_ref, acc_ref = refs
    qi = pl.program_id(2)
    seq = k_ref.shape[0]
    half = HEAD_DIM
    chunk = min(seq, 512)

    @pl.when(qi == 0)
    def _():
        vt_ref[LANES:, :] = jnp.ones((vt_ref.shape[0] - LANES, seq), BF16)
        for c in range(seq // chunk):
            sl = slice(c * chunk, (c + 1) * chunk)
            vt_ref[0:LANES, sl] = v_ref[sl, :].astype(F32).T.astype(BF16)
            if fox:
                fc = fcol_ref[0, 0, sl, :] * LOG2E
                fkc_ref[0, sl, :] = jnp.broadcast_to(fc[:, 0:1], (chunk, LANES))
                fkc_ref[1, sl, :] = jnp.broadcast_to(fc[:, 1:2], (chunk, LANES))

    qt = q_ref[...].astype(F32).T
    low = lax.broadcasted_iota(jnp.int32, (LANES, tq), 0) < half
    qst = jnp.concatenate([jnp.where(low, qt, 0.0), jnp.where(low, 0.0, qt)], axis=1).astype(BF16)

    acc_ref[...] = jnp.zeros(acc_ref.shape, F32)
    q0 = pl.multiple_of(qi * tq, tq)
    cw = tq // n_split
    n_chain = 2 * n_split
    lanes = [slice(c * cw, (c + 1) * cw) for c in range(n_chain)]
    if fox:
        shift = [frow_ref[0, 0, c // n_split:c // n_split + 1, pl.ds(q0 + (c % n_split) * cw, cw)] * LOG2E
                 for c in range(n_chain)]
    n_rep = cw // LANES

    def chain_mode(c, tile):
        g = c % n_split
        return "full" if (tile is None or g > tile) else ("diag" if g == tile else "skip")

    def scores(ki, s_ref, tile=None):
        k = k_ref[pl.ds(pl.multiple_of(ki * tk, tk), tk), :]
        for c in range(n_chain):
            if chain_mode(c, tile) != "skip":
                s_ref[:, lanes[c]] = _dot(k, qst[:, lanes[c]])

    def softmax_pv(ki, s_ref, ms, tile=None):
        start = pl.multiple_of(ki * tk, tk)
        vt = vt_ref[:, pl.ds(start, tk)]
        new = []
        for c in range(n_chain):
            mode = chain_mode(c, tile)
            if mode == "skip":
                new.append(ms[c])
                continue
            st = s_ref[:, lanes[c]]
            if fox:
                st = st - _rep(fkc_ref[c // n_split, pl.ds(start, tk), :], n_rep)
            if mode == "diag":
                visible = (lax.broadcasted_iota(jnp.int32, (tk, cw), 0)
                           <= lax.broadcasted_iota(jnp.int32, (tk, cw), 1))
                st = jnp.where(visible, st, NEG_INF)
            m_cur = jnp.max(st, axis=0, keepdims=True)
            if fox:
                m_cur = m_cur + shift[c]
            m_new = jnp.maximum(ms[c], m_cur)
            alpha = jnp.exp2(ms[c] - m_new)
            p = jnp.exp2(st - ((m_new - shift[c]) if fox else m_new))
            acc_ref[:, lanes[c]] = alpha * acc_ref[:, lanes[c]] + _dot(vt, p.astype(BF16))
            new.append(m_new)
        return tuple(new)

    def two_tiles(j, ms, diagonal):
        scores(2 * j + 1, s1_ref, 1 if diagonal else None)
        ms = softmax_pv(2 * j, s0_ref, ms, 0 if diagonal else None)
        if not diagonal:
            scores(2 * j + 2, s0_ref)
        return softmax_pv(2 * j + 1, s1_ref, ms, 1 if diagonal else None)

    assert cw == tk and tq == 2 * tk
    scores(0, s0_ref)
    ms = lax.fori_loop(0, qi, lambda j, c: two_tiles(j, c, False), (jnp.full((1, cw), NEG_INF, F32),) * n_chain)
    two_tiles(qi, ms, True)

    acc = acc_ref[...]
    out = acc[0:LANES, :] / acc[LANES:LANES + 1, :]
    if fox:
        res = jnp.where(low, out[:, 0:tq], out[:, tq:])
    else:
        lam = _diff_lambda(lqk_ref, lam_init)
        o = out[:, 0:tq] - lam * out[:, tq:]
        ms = jnp.mean(o * o, axis=0, keepdims=True)
        res = o * lax.rsqrt(ms + RMS_EPS) * gcol_ref[...] * (1.0 - lam_init)
    o_ref[...] = res.T.astype(BF16)


def _flash(q, k, v, extra, *, batch, seq, tq, tk, fox, lam_init=0.0):
    m, w = q.shape
    n_blk = w // LANES
    n_q = seq // tq
    q_spec = pl.BlockSpec((tq, LANES), lambda b, h, i: (b * n_q + i, h))
    kv_spec = pl.BlockSpec((seq, LANES), lambda b, h, i: (b, h))
    assert tq == 2 * tk
    ones_rows = 16
    scratch = [pltpu.VMEM((LANES + ones_rows, seq), BF16)]
    if fox:
        extra_specs = [pl.BlockSpec((1, 1, 2, seq), lambda b, h, i: (b, h, 0, 0)),
                       pl.BlockSpec((1, 1, seq, 2), lambda b, h, i: (h, b, 0, 0))]
        scratch.append(pltpu.VMEM((2, seq, LANES), F32))
    else:
        extra_specs = [_const_spec(extra[0].shape), _const_spec(extra[1].shape)]
    scratch += [pltpu.VMEM((tk, 2 * tq), F32), pltpu.VMEM((tk, 2 * tq), F32),
                pltpu.VMEM((LANES + ones_rows, 2 * tq), F32)]
    return pl.pallas_call(
        functools.partial(_flash_body, tq=tq, tk=tk, n_split=max(1, tq // 256), fox=fox, lam_init=lam_init),
        out_shape=jax.ShapeDtypeStruct((m, w), BF16),
        grid=(batch, n_blk, n_q),
        in_specs=[q_spec, kv_spec, kv_spec] + extra_specs,
        out_specs=q_spec,
        scratch_shapes=scratch,
        compiler_params=_cparams("parallel", "parallel", "arbitrary"),
        name="flash_fox" if fox else "flash_diff",
    )(q, k, v, *extra)


def _fcum_prompt_body(lf_ref, o_ref, *, steps):
    o_ref[...] = _lane_cumsum(lf_ref[...], steps)


def _fcum_prompt(logft, *, batch, seq):
    spec = pl.BlockSpec((8, seq), lambda b: (0, b))
    return pl.pallas_call(
        functools.partial(_fcum_prompt_body, steps=(seq - 1).bit_length()),
        out_shape=jax.ShapeDtypeStruct(logft.shape, F32),
        grid=(batch,),
        in_specs=[spec],
        out_specs=spec,
        compiler_params=_cparams("parallel"),
        name="fcum_prompt",
    )(logft)


def _fcum_past_body(pt_ref, *refs, pages):
    in_refs = refs[:pages]
    o_ref, x_ref, carry_ref = refs[pages:]
    depth, _, n_heads, page = in_refs[0].shape
    grp = depth * n_heads
    g = pl.program_id(1)

    @pl.when(g == 0)
    def _():
        carry_ref[...] = jnp.zeros(carry_ref.shape, F32)

    for j in range(pages):
        for d in range(depth):
            x_ref[pl.ds(j * grp + d * n_heads, n_heads), :] = in_refs[j][d, 0]
    incl = _lane_cumsum(x_ref[...], int(math.log2(page)))
    tot = jnp.broadcast_to(incl[:, page - 1:page], incl.shape)
    row = lax.broadcasted_iota(jnp.int32, incl.shape, 0)
    pre = tot
    s = grp
    while s < pages * grp:
        pre = pre + jnp.where(row >= s, pltpu.roll(pre, s, 0), 0.0)
        s *= 2
    carry = carry_ref[...]
    x_ref[...] = incl + (pre - tot) + jnp.concatenate([carry] * pages, axis=0)
    carry_ref[...] = carry + pre[(pages - 1) * grp:, :]
    for j in range(pages):
        for d in range(depth):
            o_ref[d, 0, j] = x_ref[pl.ds(j * grp + d * n_heads, n_heads), :]


def _fcum_past(logf_t, page_table, *, pages):
    depth, n_pool, n_heads, page = logf_t.shape
    n_seq, n_pages = page_table.shape
    in_specs = [pl.BlockSpec((depth, 1, n_heads, page),
                             functools.partial(lambda b, g, pt, j: (0, pt[b, g * pages + j], 0, 0), j=j))
                for j in range(pages)]
    grid_spec = pltpu.PrefetchScalarGridSpec(
        num_scalar_prefetch=1,
        grid=(n_seq, n_pages // pages),
        in_specs=in_specs,
        out_specs=pl.BlockSpec((depth, 1, pages, n_heads, page), lambda b, g, pt: (0, b, g, 0, 0)),
        scratch_shapes=[pltpu.VMEM((pages * depth * n_heads, page), F32),
                        pltpu.VMEM((depth * n_heads, page), F32)])
    return pl.pallas_call(
        functools.partial(_fcum_past_body, pages=pages),
        out_shape=jax.ShapeDtypeStruct((depth, n_seq, n_pages, n_heads, page), F32),
        grid_spec=grid_spec,
        compiler_params=_cparams("parallel", "arbitrary"),
        name="fcum_past",
    )(page_table, *([logf_t] * pages))


def _softmax_step(s, pv, m_ref, l_ref, acc_ref, shift=None):
    n_k = s.shape[1] // LANES
    n_v = acc_ref.shape[1] // LANES
    m_prev = m_ref[...]
    m_cur = jnp.max(s, axis=1, keepdims=True)
    if shift is not None:
        m_cur = m_cur + shift
    m_new = jnp.maximum(m_prev, m_cur)
    alpha = jnp.exp(m_prev - m_new)
    sub = m_new if shift is None else m_new - shift
    p = jnp.exp(s - _rep(sub, n_k))
    l_ref[...] = alpha * l_ref[...] + jnp.sum(p, axis=1, keepdims=True)
    acc_ref[...] = _rep(alpha, n_v) * acc_ref[...] + pv(p.astype(BF16))
    m_ref[...] = m_new


def _pad_rows(x, rows):
    return jnp.concatenate([x, jnp.zeros((rows - x.shape[0], x.shape[1]), x.dtype)], axis=0)


def _paged_copies(pt_ref, caches, bufs, sems, *, layer, pages, seq, step, slot):
    return [pltpu.make_async_copy(cache.at[layer, pt_ref[seq, step * pages + j]], buf.at[slot, j],
                                  sems.at[a, slot])
            for a, (cache, buf) in enumerate(zip(caches, bufs)) for j in range(pages)]


def _paged_pipeline(pt_ref, caches, bufs, sems, *, layer, pages):
    b, g = pl.program_id(0), pl.program_id(1)
    n_b, n_g = pl.num_programs(0), pl.num_programs(1)
    s = b * n_g + g
    slot = lax.rem(s, 2)
    copies = functools.partial(_paged_copies, pt_ref, caches, bufs, sems, layer=layer, pages=pages)

    @pl.when(s == 0)
    def _():
        for cp in copies(seq=b, step=g, slot=slot):
            cp.start()

    @pl.when(s + 1 < n_b * n_g)
    def _():
        wrap = g + 1 == n_g
        for cp in copies(seq=jnp.where(wrap, b + 1, b), step=jnp.where(wrap, 0, g + 1), slot=1 - slot):
            cp.start()

    for cp in copies(seq=b, step=g, slot=slot):
        cp.wait()
    return slot


def _decode_diff_body(pt_ref, q_ref, knew_ref, vnew_ref, lqk_ref, g_ref, k_hbm, v_hbm, o_ref,
                      kbuf_ref, vbuf_ref, sem_ref, qs_ref, m_ref, l_ref, acc_ref,
                      *, layer, pages, n_heads, t_new, lam_init):
    slot = _paged_pipeline(pt_ref, (k_hbm, v_hbm), (kbuf_ref, vbuf_ref), sem_ref, layer=layer, pages=pages)
    g = pl.program_id(1)
    page = kbuf_ref.shape[2] // n_heads
    hw = 2 * HEAD_DIM
    rows_h = 2 * t_new

    @pl.when(g == 0)
    def _():
        q = q_ref[0]
        low = lax.broadcasted_iota(jnp.int32, (t_new, hw), 1) < HEAD_DIM
        pieces = []
        for h in range(n_heads):
            qh = q[:, h * hw:(h + 1) * hw]
            pieces += [jnp.where(low, qh, 0.0), jnp.where(low, 0.0, qh)]
        qs_ref[...] = jnp.concatenate(pieces, axis=0).astype(BF16)
        m_ref[...] = jnp.full(m_ref.shape, NEG_INF, F32)
        l_ref[...] = jnp.zeros(l_ref.shape, F32)
        acc_ref[...] = jnp.zeros(acc_ref.shape, F32)

    def head_rows(buf_ref, h):
        return jnp.concatenate([buf_ref[slot, j, pl.ds(h, page, stride=n_heads), :].astype(BF16)
                                for j in range(pages)], axis=0)

    def per_head(fn):
        return jnp.concatenate([fn(h, slice(h * rows_h, (h + 1) * rows_h)) for h in range(n_heads)], axis=0)

    qs = qs_ref[...]
    vs = [head_rows(vbuf_ref, h) for h in range(n_heads)]
    s = per_head(lambda h, rows: _dot_nt(qs[rows, :], head_rows(kbuf_ref, h)))
    _softmax_step(s, lambda p: per_head(lambda h, rows: _dot(p[rows, :], vs[h])), m_ref, l_ref, acc_ref)

    @pl.when(g == pl.num_programs(1) - 1)
    def _():
        k_new = knew_ref[0]
        v_new = vnew_ref[0]
        kn = [_pad_rows(k_new[:, h * hw:(h + 1) * hw], LANES).astype(BF16) for h in range(n_heads)]
        vn = [_pad_rows(v_new[:, h * hw:(h + 1) * hw], LANES).astype(BF16) for h in range(n_heads)]
        s_new = per_head(lambda h, rows: _dot_nt(qs[rows, :], kn[h]))
        row = lax.broadcasted_iota(jnp.int32, s_new.shape, 0) & (t_new - 1)
        col = lax.broadcasted_iota(jnp.int32, s_new.shape, 1)
        s_new = jnp.where(col <= row, s_new, NEG_INF)
        _softmax_step(s_new, lambda p: per_head(lambda h, rows: _dot(p[rows, :], vn[h])), m_ref, l_ref, acc_ref)

        out = acc_ref[...] / l_ref[...]
        lam = _diff_lambda(lqk_ref, lam_init)
        for h in range(n_heads):
            o = out[h * rows_h:h * rows_h + t_new, :] - lam * out[h * rows_h + t_new:(h + 1) * rows_h, :]
            ms = jnp.mean(o * o, axis=-1, keepdims=True)
            o_ref[0, :, h * hw:(h + 1) * hw] = o * lax.rsqrt(ms + RMS_EPS) * g_ref[...] * (1.0 - lam_init)


def _decode_diff(q, k_new, v_new, cache_k, cache_v, page_table, lqk, g, *, layer, pages, lam_init):
    n_seq, t_new, width = q.shape
    n_heads = width // (2 * HEAD_DIM)
    n_pages = page_table.shape[1]
    rows = cache_k.shape[2]
    seq_spec = pl.BlockSpec((1, t_new, width), lambda b, g_, pt: (b, 0, 0))
    hbm = pl.BlockSpec(memory_space=pl.ANY)
    n_rows = n_heads * 2 * t_new
    grid_spec = pltpu.PrefetchScalarGridSpec(
        num_scalar_prefetch=1,
        grid=(n_seq, n_pages // pages),
        in_specs=[seq_spec, seq_spec, seq_spec,
                  pl.BlockSpec(lqk.shape, lambda b, g_, pt: (0, 0)),
                  pl.BlockSpec(g.shape, lambda b, g_, pt: (0, 0)), hbm, hbm],
        out_specs=seq_spec,
        scratch_shapes=[pltpu.VMEM((2, pages, rows, LANES), F32), pltpu.VMEM((2, pages, rows, LANES), F32),
                        pltpu.SemaphoreType.DMA((2, 2)),
                        pltpu.VMEM((n_rows, LANES), BF16), pltpu.VMEM((n_rows, LANES), F32),
                        pltpu.VMEM((n_rows, LANES), F32), pltpu.VMEM((n_rows, LANES), F32)])
    return pl.pallas_call(
        functools.partial(_decode_diff_body, layer=layer, pages=pages, n_heads=n_heads, t_new=t_new,
                          lam_init=lam_init),
        out_shape=jax.ShapeDtypeStruct((n_seq, t_new, width), F32),
        grid_spec=grid_spec,
        compiler_params=_cparams("arbitrary", "arbitrary"),
        name="decode_diff",
    )(page_table, q, k_new, v_new, lqk, g, cache_k, cache_v)


def _decode_fox_body(pt_ref, q_ref, knew_ref, vnew_ref, lfnew_ref, fk_ref, fklast_ref, kt_hbm, vt_hbm, o_ref,
                     ktbuf_ref, vtbuf_ref, sem_ref, qs_ref, m_ref, l_ref, acc_ref,
                     *, layer, pages, n_heads, t_new):
    slot = _paged_pipeline(pt_ref, (kt_hbm, vt_hbm), (ktbuf_ref, vtbuf_ref), sem_ref, layer=layer, pages=pages)
    g = pl.program_id(1)
    width = q_ref.shape[2]
    page = ktbuf_ref.shape[3]
    n_rows = n_heads * t_new

    @pl.when(g == 0)
    def _():
        q = q_ref[0]
        lane = lax.broadcasted_iota(jnp.int32, (t_new, width), 1)
        qs_ref[...] = jnp.concatenate(
            [jnp.where((lane >= h * HEAD_DIM) & (lane < (h + 1) * HEAD_DIM), q, 0.0) for h in range(n_heads)],
            axis=0).astype(BF16)
        m_ref[...] = jnp.full(m_ref.shape, NEG_INF, F32)
        l_ref[...] = jnp.zeros(l_ref.shape, F32)
        acc_ref[...] = jnp.zeros(acc_ref.shape, F32)

    def head_rows(per_head):
        return jnp.concatenate([jnp.broadcast_to(per_head(h), (t_new, per_head(h).shape[1]))
                                for h in range(n_heads)], axis=0)

    total = fklast_ref[0, 0, 0, :, page - 1:page]
    f_new = _lane_cumsum(lfnew_ref[0], int(math.log2(t_new))) + jnp.concatenate(
        [total, jnp.zeros((8 - n_heads, 1), F32)], axis=0)
    eye = (lax.broadcasted_iota(jnp.int32, (t_new, LANES), 0)
           == lax.broadcasted_iota(jnp.int32, (t_new, LANES), 1))
    fq_col = jnp.concatenate(
        [jnp.sum(jnp.where(eye, jnp.broadcast_to(f_new[h:h + 1, :], (t_new, LANES)), 0.0),
                 axis=1, keepdims=True) for h in range(n_heads)], axis=0)
    shift = jnp.broadcast_to(fq_col, (n_rows, LANES))

    n_chain = m_ref.shape[0]
    per = pages // n_chain
    qs = qs_ref[...]
    scores = []
    for c in range(n_chain):
        sel = range(c * per, (c + 1) * per)
        kt = jnp.concatenate([ktbuf_ref[slot, j].astype(BF16) for j in sel], axis=1)
        fk = jnp.concatenate([head_rows(lambda h: fk_ref[0, 0, j, h:h + 1, :]) for j in sel], axis=1)
        scores.append(_dot(qs, kt) - fk)
    for c in range(n_chain):
        vt = jnp.concatenate([vtbuf_ref[slot, j].astype(BF16) for j in range(c * per, (c + 1) * per)], axis=1)
        _softmax_step(scores[c], lambda p, vt=vt: _dot_nt(p, vt), m_ref.at[c], l_ref.at[c], acc_ref.at[c], shift)

    @pl.when(g == pl.num_programs(1) - 1)
    def _():
        k_new = _pad_rows(knew_ref[0], LANES).astype(BF16)
        v_new = _pad_rows(vnew_ref[0], LANES).astype(BF16)
        s_new = _dot_nt(qs, k_new) - head_rows(lambda h: f_new[h:h + 1, :])
        row = lax.broadcasted_iota(jnp.int32, s_new.shape, 0) & (t_new - 1)
        col = lax.broadcasted_iota(jnp.int32, s_new.shape, 1)
        s_new = jnp.where(col <= row, s_new, NEG_INF)
        _softmax_step(s_new, lambda p: _dot(p, v_new), m_ref.at[0], l_ref.at[0], acc_ref.at[0], shift)

        m_all = m_ref[0]
        for c in range(1, n_chain):
            m_all = jnp.maximum(m_all, m_ref[c])
        l_all = jnp.zeros(m_all.shape, F32)
        acc_all = jnp.zeros(acc_ref.shape[1:], F32)
        for c in range(n_chain):
            wgt = jnp.exp(m_ref[c] - m_all)
            l_all = l_all + wgt * l_ref[c]
            acc_all = acc_all + _rep(wgt, width // LANES) * acc_ref[c]
        out = acc_all / _rep(l_all, width // LANES)
        lane = lax.broadcasted_iota(jnp.int32, (t_new, width), 1)
        res = jnp.zeros((t_new, width), F32)
        for h in range(n_heads):
            keep = (lane >= h * HEAD_DIM) & (lane < (h + 1) * HEAD_DIM)
            res = jnp.where(keep, out[h * t_new:(h + 1) * t_new, :], res)
        o_ref[0] = res


def _decode_fox(q, k_new, v_new, cache_kt, cache_vt, page_table, lf_new, fk, *, layer, pages):
    n_seq, t_new, width = q.shape
    n_heads = width // HEAD_DIM
    page = cache_kt.shape[3]
    n_pages = page_table.shape[1]
    seq_spec = pl.BlockSpec((1, t_new, width), lambda b, g, pt: (b, 0, 0))
    hbm = pl.BlockSpec(memory_space=pl.ANY)
    n_rows = n_heads * t_new
    n_chain = 2 if pages % 2 == 0 else 1
    grid_spec = pltpu.PrefetchScalarGridSpec(
        num_scalar_prefetch=1,
        grid=(n_seq, n_pages // pages),
        in_specs=[seq_spec, seq_spec, seq_spec,
                  pl.BlockSpec((1, 8, LANES), lambda b, g, pt: (b, 0, 0)),
                  pl.BlockSpec((1, 1, pages, n_heads, page), lambda b, g, pt: (layer, b, g, 0, 0)),
                  pl.BlockSpec((1, 1, 1, n_heads, page), lambda b, g, pt: (layer, b, n_pages - 1, 0, 0)),
                  hbm, hbm],
        out_specs=seq_spec,
        scratch_shapes=[pltpu.VMEM((2, pages, width, page), F32), pltpu.VMEM((2, pages, width, page), F32),
                        pltpu.SemaphoreType.DMA((2, 2)),
                        pltpu.VMEM((n_rows, width), BF16), pltpu.VMEM((n_chain, n_rows, LANES), F32),
                        pltpu.VMEM((n_chain, n_rows, LANES), F32), pltpu.VMEM((n_chain, n_rows, width), F32)])
    return pl.pallas_call(
        functools.partial(_decode_fox_body, layer=layer, pages=pages, n_heads=n_heads, t_new=t_new),
        out_shape=jax.ShapeDtypeStruct((n_seq, t_new, width), F32),
        grid_spec=grid_spec,
        compiler_params=_cparams("arbitrary", "arbitrary"),
        name="decode_fox",
    )(page_table, q, k_new, v_new, lf_new, fk, fk, cache_kt, cache_vt)


def _merge_body(x_ref, yad_ref, yb_ref, yc_ref, wg_ref, wp_ref, wo_ref, g_ref, b_ref, o_ref, *, alpha):
    x = x_ref[...]
    xb = x.astype(BF16)
    d = x.shape[1]
    w = d // 4
    yad = yad_ref[...]
    ys = (yad[:, 0:w], yb_ref[...], yc_ref[...], yad[:, w:2 * w])
    merged = None
    row = 0
    for i, y in enumerate(ys):
        gate = jax.nn.sigmoid(_dot(xb, wg_ref[:, i * d:(i + 1) * d]))
        term = gate * _dot(y.astype(BF16), wp_ref[row:row + y.shape[1], :])
        merged = term if merged is None else merged + term
        row += y.shape[1]
    out = _dot(merged.astype(BF16), wo_ref[...])
    o_ref[...] = _layer_norm(alpha * x + out, g_ref[...], b_ref[...])


def _merge(x, yad, yb, yc, wg, wp, wo, g, b, *, layer, tm, alpha):
    m, d = x.shape
    row = lambda w: pl.BlockSpec((tm, w), lambda i: (i, 0))
    return pl.pallas_call(
        functools.partial(_merge_body, alpha=alpha),
        out_shape=jax.ShapeDtypeStruct((m, d), F32),
        grid=(m // tm,),
        in_specs=[row(d), row(yad.shape[1]), row(yb.shape[1]), row(yc.shape[1]),
                  _layer_spec(wg, layer), _layer_spec(wp, layer), _layer_spec(wo, layer),
                  _const_spec((1, d)), _const_spec((1, d))],
        out_specs=row(d),
        compiler_params=_cparams("parallel"),
        name="merge",
    )(x, yad, yb, yc, wg, wp, wo, g, b)


def _rope_tables(pos):
    half = HEAD_DIM // 2
    inv = ROPE_THETA ** (-jnp.arange(half, dtype=F32) / half)
    ang = pos.astype(F32)[:, None] * inv[None, :]
    cos, sin = jnp.cos(ang), jnp.sin(ang)
    cos_t = jnp.tile(cos, (1, LANES // half))
    sin_t = jnp.tile(jnp.concatenate([-sin, sin], axis=1), (1, LANES // HEAD_DIM))
    return cos_t, sin_t


def _pick_tile(n, target):
    t = min(n, target)
    while n % t:
        t //= 2
    return t


def kernel(x_prompt, x_sample, cache_diff_k, cache_diff_v, cache_fox_k, cache_fox_v, cache_fox_logf,
           state_pool, state_conv, page_table, w_in, b_fgate, pool_w, pool_scale, lambda_qk, subln_g,
           conv_w, w_branch, w_o, w_ffn1_gate, w_ffn1_up, w_ffn1_down, w_ffn2_gate, w_ffn2_up,
           w_ffn2_down, ln_g, ln_b):
    batch, seq, d = x_prompt.shape
    n_seq, t_new, _ = x_sample.shape
    depth = w_in.shape[0]
    n_pool, page = cache_diff_k.shape[1:3]
    past_len = page_table.shape[1] * page
    h_diff = cache_diff_k.shape[3]
    h_fox = cache_fox_k.shape[3]
    w_pool = state_pool.shape[-1]
    w_diff, w_fox, w_conv = d // 2, d // 4, d // 4
    alpha = (2.0 * depth) ** 0.25

    sizes = (w_pool, w_diff, w_diff, w_diff, w_fox, w_fox, w_fox, h_fox, w_conv, w_conv, w_conv, 4 * d)
    offs = np.concatenate([[0], np.cumsum(sizes)])
    col = lambda i: w_in[:, :, offs[i]:offs[i + 1]]
    w_main = jnp.concatenate([col(0), col(8), col(9), col(10), col(1), col(2), col(3), col(4), col(5), col(6)],
                             axis=-1).astype(BF16)
    w_ft = jnp.pad(jnp.swapaxes(col(7), 1, 2), ((0, 0), (0, 16 - h_fox), (0, 0))).astype(BF16)
    b_f = jnp.pad(b_fgate, ((0, 0), (0, 8 - h_fox)))[:, :, None]
    w_gate = col(11).astype(BF16)
    w_br = w_branch.astype(BF16)
    w_out = w_o.astype(BF16)
    ffn = [tuple(w.astype(BF16) for w in ws) for ws in
           ((w_ffn1_gate, w_ffn1_up, w_ffn1_down), (w_ffn2_gate, w_ffn2_up, w_ffn2_down))]
    n_grp, grp = pool_w.shape[1], pool_w.shape[2]
    wbd = jnp.zeros((depth, w_pool, w_pool), F32)
    for gidx in range(n_grp):
        wbd = wbd.at[:, gidx * grp:(gidx + 1) * grp, gidx * grp:(gidx + 1) * grp].set(pool_w[:, gidx])
    wbd = wbd.astype(BF16)
    pscale = pool_scale[:, None, :]

    cos_p, sin_p = _rope_tables(jnp.arange(seq, dtype=jnp.int32))
    cos_s, sin_s = _rope_tables(past_len + jnp.arange(t_new, dtype=jnp.int32))
    cos_s, sin_s = jnp.tile(cos_s, (n_seq, 1)), jnp.tile(sin_s, (n_seq, 1))

    ck_diff = cache_diff_k.reshape(depth, n_pool, page * h_diff, 2 * HEAD_DIM)
    cv_diff = cache_diff_v.reshape(depth, n_pool, page * h_diff, 2 * HEAD_DIM)
    ck_fox = jnp.transpose(cache_fox_k, (0, 1, 3, 4, 2)).reshape(depth, n_pool, h_fox * HEAD_DIM, page)
    cv_fox = jnp.transpose(cache_fox_v, (0, 1, 3, 4, 2)).reshape(depth, n_pool, h_fox * HEAD_DIM, page)
    pages = _pick_tile(page_table.shape[1], 16)
    fk_past = _fcum_past(jnp.transpose(cache_fox_logf, (0, 1, 3, 2)), page_table, pages=pages)

    halo_p = jnp.zeros((batch, HALO_ROWS, 2 * w_pool), F32)

    tm_p = _pick_tile(batch * seq, 512)
    tm_s = n_seq * t_new
    tq = _pick_tile(seq, 512)
    tk = tq // 2

    pages_fox = _pick_tile(page_table.shape[1], 32)
    p_kv = [jnp.zeros((depth, batch * seq * h_diff, 2 * HEAD_DIM), F32) for _ in range(2)]
    p_kv += [jnp.zeros((depth, batch, h_fox * HEAD_DIM, seq), F32) for _ in range(2)]

    xp = x_prompt.reshape(batch * seq, d)
    xs = x_sample.reshape(n_seq * t_new, d)
    p_rows, s_rows = [], []
    for l in range(depth):
        lam_init = 0.8 - 0.6 * math.exp(-0.3 * l)
        lng = lambda i: ln_g[l, i][None, :]
        lnb = lambda i: ln_b[l, i][None, :]
        halo_s = jnp.concatenate(
            [jnp.pad(state_pool[l], ((0, 0), (HALO_ROWS - state_pool.shape[2], 0), (0, 0))),
             jnp.pad(state_conv[l], ((0, 0), (HALO_ROWS - state_conv.shape[2], 0), (0, 0)))], axis=-1)

        xp = _ffn_ln(xp, *ffn[0], lng(0), lnb(0), layer=l, tm=tm_p, alpha=alpha)
        yad, state, qd, kd16, vd16, qf, kf16, vf16, logft, *p_kv = _in_proj(
            xp, w_main, w_ft, b_f, cos_p, sin_p, layer=l, tm=tm_p,
            q_scale=LOG2E / math.sqrt(HEAD_DIM),
            stacked=(halo_p, wbd, pscale, conv_w, *p_kv))
        yb = _flash(qd, kd16, vd16, (lambda_qk[l], subln_g[l][:, None]), batch=batch, seq=seq, tq=tq, tk=tk,
                    fox=False, lam_init=lam_init)
        logf = logft[:h_fox].reshape(h_fox, batch, seq)
        f_cum = _fcum_prompt(logft, batch=batch, seq=seq)[:h_fox].reshape(h_fox // 2, 2, batch, seq)
        f_row = jnp.transpose(f_cum, (2, 0, 1, 3))
        f_col = jnp.transpose(f_cum, (0, 2, 3, 1))
        yc = _flash(qf, kf16, vf16, (f_row, f_col), batch=batch, seq=seq, tq=tq, tk=tk, fox=True)
        xp = _merge(xp, yad, yb, yc, w_gate, w_br, w_out, lng(1), lnb(1), layer=l, tm=tm_p, alpha=alpha)
        xp = _ffn_ln(xp, *ffn[1], lng(2), lnb(2), layer=l, tm=tm_p, alpha=alpha)
        p_rows.append((jnp.transpose(logf, (1, 2, 0)),
                       state[:, 1:, :w_pool], state[:, HALO_ROWS - state_conv.shape[2]:, w_pool:]))

        xs = _ffn_ln(xs, *ffn[0], lng(0), lnb(0), layer=l, tm=tm_s, alpha=alpha)
        u_mix, qd, _, _, qf, _, _, logft, kd32, vd32, kf32, vf32 = _in_proj(
            xs, w_main, w_ft, b_f, cos_s, sin_s, layer=l, tm=tm_s, q_scale=1.0 / math.sqrt(HEAD_DIM))
        yad, state = _mixers(u_mix.reshape(n_seq, t_new, d), halo_s, wbd, pscale, conv_w,
                             layer=l, tm=t_new, pos0=past_len)
        yad = yad.reshape(n_seq * t_new, 2 * w_pool)
        seq3 = lambda a: a.reshape(n_seq, t_new, a.shape[-1])
        yb = _decode_diff(seq3(qd.astype(F32)), seq3(kd32), seq3(vd32), ck_diff, cv_diff, page_table,
                          lambda_qk[l], subln_g[l][None, :], layer=l, pages=pages, lam_init=lam_init)
        logf = logft[:h_fox].reshape(h_fox, n_seq, t_new)
        lf_new = jnp.pad(jnp.transpose(logf, (1, 0, 2)), ((0, 0), (0, 8 - h_fox), (0, LANES - t_new)))
        yc = _decode_fox(seq3(qf.astype(F32)), seq3(kf32), seq3(vf32), ck_fox, cv_fox, page_table,
                         lf_new, fk_past, layer=l, pages=pages_fox)
        xs = _merge(xs, yad, yb.reshape(n_seq * t_new, -1), yc.reshape(n_seq * t_new, -1),
                    w_gate, w_br, w_out, lng(1), lnb(1), layer=l, tm=tm_s, alpha=alpha)
        xs = _ffn_ln(xs, *ffn[1], lng(2), lnb(2), layer=l, tm=tm_s, alpha=alpha)
        s_rows.append((kd32.reshape(n_seq, t_new, h_diff, -1), vd32.reshape(n_seq, t_new, h_diff, -1),
                       kf32.reshape(n_seq, t_new, h_fox, -1), vf32.reshape(n_seq, t_new, h_fox, -1),
                       jnp.transpose(logf, (1, 2, 0)),
                       state[:, 1:, :w_pool], state[:, HALO_ROWS - state_conv.shape[2]:, w_pool:]))

    p_small = [jnp.stack([r[i] for r in p_rows], axis=0) for i in range(3)]
    s_out = [jnp.stack([r[i] for r in s_rows], axis=0) for i in range(7)]
    kd_all, vd_all, kf_all, vf_all = p_kv
    fox_rows = lambda a: jnp.transpose(a.reshape(depth, batch, h_fox, HEAD_DIM, seq), (0, 1, 4, 2, 3))
    return (xp.reshape(batch, seq, d), xs.reshape(n_seq, t_new, d),
            kd_all.reshape(depth, batch, seq, h_diff, 2 * HEAD_DIM),
            vd_all.reshape(depth, batch, seq, h_diff, 2 * HEAD_DIM),
            fox_rows(kf_all), fox_rows(vf_all), *p_small, *s_out)
```

```python
import functools
import math

import jax
import jax.numpy as jnp
import numpy as np
from jax import lax
from jax.experimental import pallas as pl
from jax.experimental.pallas import tpu as pltpu

F32 = jnp.float32
BF16 = jnp.bfloat16

HEAD_DIM = 64
ROPE_THETA = 10000.0
LN_EPS = 1e-5
RMS_EPS = 1e-5
NEG_INF = -1e30
LOG2E = math.log2(math.e)
POOL_WINDOWS = (2, 4, 8, 16)
HALO_ROWS = 16
LANES = 128
VMEM_LIMIT_BYTES = 56 * 1024 * 1024


def _cparams(*sem):
    return pltpu.CompilerParams(dimension_semantics=sem, vmem_limit_bytes=VMEM_LIMIT_BYTES)


def _const_spec(shape):
    zeros = (0,) * len(shape)
    return pl.BlockSpec(shape, lambda *_: zeros, pipeline_mode=pl.Buffered(1))


def _layer_spec(arr, layer):
    zeros = (0,) * (arr.ndim - 1)
    return pl.BlockSpec((None,) + arr.shape[1:], lambda *_: (layer,) + zeros, pipeline_mode=pl.Buffered(1))


def _dot(a, b):
    return jnp.dot(a, b, preferred_element_type=F32)


def _dot_nt(a, b):
    return lax.dot_general(a, b, (((1,), (1,)), ((), ())), preferred_element_type=F32)


def _rep(x, n):
    return x if n == 1 else jnp.concatenate([x] * n, axis=1)


def _layer_norm(z, g, b):
    mu = jnp.mean(z, axis=-1, keepdims=True)
    zc = z - mu
    var = jnp.mean(zc * zc, axis=-1, keepdims=True)
    return zc * lax.rsqrt(var + LN_EPS) * g + b


def _ffn_chunk(xb, wg_ref, wu_ref, wd_ref, acc_ref, c, chunk):
    sl = slice(c * chunk, (c + 1) * chunk)
    gate = _dot(xb, wg_ref[:, sl])
    up = _dot(xb, wu_ref[:, sl])
    h = gate * jax.nn.sigmoid(gate) * up
    d = _dot(h.astype(BF16), wd_ref[sl, :])
    if c == 0:
        acc_ref[...] = d
    else:
        acc_ref[...] += d


def _ffn_ln_body(x_ref, wg_ref, wu_ref, wd_ref, g_ref, b_ref, o_ref, acc_ref, *, chunk, alpha):
    x = x_ref[...]
    xb = x.astype(BF16)
    for c in range(wg_ref.shape[1] // chunk):
        _ffn_chunk(xb, wg_ref, wu_ref, wd_ref, acc_ref, c, chunk)
    z = alpha * x + 0.5 * acc_ref[...]
    o_ref[...] = _layer_norm(z, g_ref[...], b_ref[...])


def _ffn_ln(x, wg, wu, wd, g, b, *, layer, tm, alpha):
    m, d = x.shape
    row = pl.BlockSpec((tm, d), lambda i: (i, 0))
    return pl.pallas_call(
        functools.partial(_ffn_ln_body, chunk=256, alpha=alpha),
        out_shape=jax.ShapeDtypeStruct((m, d), F32),
        grid=(m // tm,),
        in_specs=[row, _layer_spec(wg, layer), _layer_spec(wu, layer), _layer_spec(wd, layer),
                  _const_spec((1, d)), _const_spec((1, d))],
        out_specs=row,
        scratch_shapes=[pltpu.VMEM((tm, d), F32)],
        compiler_params=_cparams("parallel"),
        name="ffn_ln",
    )(x, wg, wu, wd, g, b)


def _mix_tile(u, ext_ref, wbd, pscale, cw, pos_start):
    tm = u.shape[0]
    w = u.shape[1] // 4
    up = u[:, 0:w]
    cb = u[:, w:2 * w]
    ext_ref[HALO_ROWS:HALO_ROWS + tm, 0:w] = up
    ext_ref[HALO_ROWS:HALO_ROWS + tm, w:2 * w] = u[:, 2 * w:3 * w] * u[:, 3 * w:4 * w]
    e = ext_ref[...]
    ep = e[:, 0:w]
    ev = e[:, w:2 * w]

    sums = []
    b = ep
    for k in (1, 2, 4, 8):
        b = b + pltpu.roll(b, k, 0)
        sums.append(b[HALO_ROWS:, :])
    lane = lax.broadcasted_iota(jnp.int32, (tm, w), 1)
    grp = w // len(POOL_WINDOWS)
    win_sum = jnp.where(lane < grp, sums[0],
                        jnp.where(lane < 2 * grp, sums[1], jnp.where(lane < 3 * grp, sums[2], sums[3])))
    window = jnp.where(lane < grp, POOL_WINDOWS[0],
                       jnp.where(lane < 2 * grp, POOL_WINDOWS[1],
                                 jnp.where(lane < 3 * grp, POOL_WINDOWS[2], POOL_WINDOWS[3])))
    pos = pos_start + lax.broadcasted_iota(jnp.int32, (tm, w), 0)
    cnt = jnp.minimum(pos + 1, window).astype(F32)
    pooled = win_sum / cnt - up
    ya = _dot(pooled.astype(BF16), wbd) * pscale

    conv = cw[0:1, :] * pltpu.roll(ev, 2, 0) + cw[1:2, :] * pltpu.roll(ev, 1, 0) + cw[2:3, :] * ev
    yd = cb * conv[HALO_ROWS:, :]
    return ya, yd, e[tm:tm + HALO_ROWS, :]


def _in_proj_body(*refs, d, q_scale, stacked, n_tab):
    if stacked:
        (x_ref, w_ref, wft_ref, bf_ref, cos_ref, sin_ref, halo_ref, wbd_ref, pscale_ref, cw_ref,
         _, _, _, _,
         yad_ref, state_ref, qd_ref, kd16_ref, vd16_ref, qf_ref, kf16_ref, vf16_ref, logft_ref,
         kd32_ref, vd32_ref, kf32_ref, vf32_ref, ext_ref) = refs
    else:
        (x_ref, w_ref, wft_ref, bf_ref, cos_ref, sin_ref,
         umix_ref, qd_ref, kd16_ref, vd16_ref, qf_ref, kf16_ref, vf16_ref, logft_ref,
         kd32_ref, vd32_ref, kf32_ref, vf32_ref) = refs
    w_mix, w_diff, w_fox = d, d // 2, d // 4
    n_blk = w_diff // LANES
    xb = x_ref[...].astype(BF16)
    tm = xb.shape[0]

    def put_diff(ref, j, val):
        if stacked:
            ref[0, pl.ds(j, tm, stride=n_blk), :] = val
        else:
            ref[:, j * LANES:(j + 1) * LANES] = val

    def put_fox(ref, val):
        if stacked:
            ref[0, 0] = val.T
        else:
            ref[...] = val

    u_mix = _dot(xb, w_ref[:, 0:w_mix])
    if stacked:
        ti = pl.program_id(0) % n_tab

        @pl.when(ti == 0)
        def _():
            ext_ref[0:HALO_ROWS, :] = halo_ref[0]

        ya, yd, new_halo = _mix_tile(u_mix, ext_ref, wbd_ref[...], pscale_ref[...], cw_ref[...], ti * tm)
        yad_ref[:, 0:w_fox] = ya.astype(BF16)
        yad_ref[:, w_fox:2 * w_fox] = yd.astype(BF16)
        ext_ref[0:HALO_ROWS, :] = new_halo
        state_ref[0] = new_halo
    else:
        umix_ref[...] = u_mix

    qk = _dot(xb, w_ref[:, w_mix:w_mix + 2 * w_diff])
    cos = cos_ref[...]
    sin = sin_ref[...]
    lane = lax.broadcasted_iota(jnp.int32, (tm, LANES), 1)
    first_half = (lane & (HEAD_DIM // 2)) == 0
    for j in range(2 * n_blk):
        blk = qk[:, j * LANES:(j + 1) * LANES]
        partner = jnp.where(first_half, pltpu.roll(blk, LANES - HEAD_DIM // 2, 1),
                            pltpu.roll(blk, HEAD_DIM // 2, 1))
        r = blk * cos + partner * sin
        if j < n_blk:
            qd_ref[:, j * LANES:(j + 1) * LANES] = (r * q_scale).astype(BF16)
        else:
            put_diff(kd32_ref, j - n_blk, r)
            kd16_ref[:, (j - n_blk) * LANES:(j - n_blk + 1) * LANES] = r.astype(BF16)

    off = w_mix + 2 * w_diff
    vd = _dot(xb, w_ref[:, off:off + w_diff])
    for j in range(n_blk):
        put_diff(vd32_ref, j, vd[:, j * LANES:(j + 1) * LANES])
    vd16_ref[...] = vd.astype(BF16)

    off += w_diff
    fox = _dot(xb, w_ref[:, off:off + 3 * w_fox])
    qf_ref[...] = (fox[:, 0:w_fox] * q_scale).astype(BF16)
    kf = fox[:, w_fox:2 * w_fox]
    put_fox(kf32_ref, kf)
    kf16_ref[...] = kf.astype(BF16)
    vf = fox[:, 2 * w_fox:3 * w_fox]
    put_fox(vf32_ref, vf)
    vf16_ref[...] = vf.astype(BF16)

    ft = _dot_nt(wft_ref[...], xb)
    logft_ref[...] = jax.nn.log_sigmoid(ft[0:8, :] + bf_ref[...])


def _in_proj(x, w_main, w_ft, b_f, cos_t, sin_t, *, layer, tm, q_scale, stacked=None):
    m, d = x.shape
    n_tab = cos_t.shape[0] // tm
    w_diff, w_fox = d // 2, d // 4
    n_blk = w_diff // LANES
    row = lambda w: pl.BlockSpec((tm, w), lambda i: (i, 0))
    tab = pl.BlockSpec((tm, LANES), lambda i: (i % n_tab, 0))
    sds = jax.ShapeDtypeStruct
    out_shape = [sds((m, w_diff), BF16), sds((m, w_diff), BF16), sds((m, w_diff), BF16),
                 sds((m, w_fox), BF16), sds((m, w_fox), BF16), sds((m, w_fox), BF16), sds((8, m), F32)]
    out_specs = [row(w_diff), row(w_diff), row(w_diff), row(w_fox), row(w_fox), row(w_fox),
                 pl.BlockSpec((8, tm), lambda i: (0, i))]
    in_specs = [row(d), _layer_spec(w_main, layer), _layer_spec(w_ft, layer), _layer_spec(b_f, layer), tab, tab]
    args = [x, w_main, w_ft, b_f, cos_t, sin_t]
    aliases = {}
    scratch = []
    if stacked is None:
        out_shape = [sds((m, d), F32)] + out_shape + [sds((m, w_diff), F32), sds((m, w_diff), F32),
                                                      sds((m, w_fox), F32), sds((m, w_fox), F32)]
        out_specs = [row(d)] + out_specs + [row(w_diff), row(w_diff), row(w_fox), row(w_fox)]
    else:
        halo, wbd, pscale, cw = stacked[:4]
        bufs = list(stacked[4:])
        seq_block = lambda i: (i // n_tab, 0, 0)
        in_specs += [pl.BlockSpec((1,) + halo.shape[1:], seq_block), _layer_spec(wbd, layer),
                     _layer_spec(pscale, layer), _layer_spec(cw, layer)]
        args += [halo, wbd, pscale, cw]
        out_shape = [sds((m, 2 * w_fox), BF16), sds(halo.shape, F32)] + out_shape
        out_specs = [row(2 * w_fox), pl.BlockSpec((1,) + halo.shape[1:], seq_block)] + out_specs
        diff_spec = pl.BlockSpec((1, tm * n_blk, LANES), lambda i: (layer, i, 0))
        fox_spec = pl.BlockSpec((1, 1, w_fox, tm), lambda i: (layer, i // n_tab, 0, i % n_tab))
        aliases = {len(args) + k: len(out_shape) + k for k in range(len(bufs))}
        out_shape += [sds(b.shape, b.dtype) for b in bufs]
        out_specs += [diff_spec, diff_spec, fox_spec, fox_spec]
        in_specs += [pl.BlockSpec(memory_space=pl.ANY)] * len(bufs)
        args += bufs
        scratch = [pltpu.VMEM((HALO_ROWS + tm, 2 * w_fox), F32)]
    return pl.pallas_call(
        functools.partial(_in_proj_body, d=d, q_scale=q_scale, stacked=stacked is not None, n_tab=n_tab),
        out_shape=out_shape,
        grid=(m // tm,),
        in_specs=in_specs,
        out_specs=out_specs,
        scratch_shapes=scratch,
        input_output_aliases=aliases,
        compiler_params=_cparams("arbitrary"),
        name="in_proj",
    )(*args)


def _mixers_body(u_ref, halo_ref, wbd_ref, pscale_ref, cw_ref, yad_ref, state_ref, ext_ref,
                 *, tm, pos0, w):
    out_dtype = yad_ref.dtype
    ti = pl.program_id(1)

    @pl.when(ti == 0)
    def _():
        ext_ref[0:HALO_ROWS, :] = halo_ref[0]

    ya, yd, new_halo = _mix_tile(u_ref[0], ext_ref, wbd_ref[...], pscale_ref[...], cw_ref[...], pos0 + ti * tm)
    yad_ref[0, :, 0:w] = ya.astype(out_dtype)
    yad_ref[0, :, w:2 * w] = yd.astype(out_dtype)
    ext_ref[0:HALO_ROWS, :] = new_halo
    state_ref[0] = new_halo


def _mixers(u_mix, halo, wbd, pscale, cw, *, layer, tm, pos0):
    b, t, d = u_mix.shape
    w = d // 4
    return pl.pallas_call(
        functools.partial(_mixers_body, tm=tm, pos0=pos0, w=w),
        out_shape=[jax.ShapeDtypeStruct((b, t, 2 * w), BF16 if tm % 16 == 0 else F32),
                   jax.ShapeDtypeStruct((b, HALO_ROWS, 2 * w), F32)],
        grid=(b, t // tm),
        in_specs=[pl.BlockSpec((1, tm, d), lambda i, j: (i, j, 0)),
                  pl.BlockSpec((1, HALO_ROWS, 2 * w), lambda i, j: (i, 0, 0)),
                  _layer_spec(wbd, layer), _layer_spec(pscale, layer), _layer_spec(cw, layer)],
        out_specs=[pl.BlockSpec((1, tm, 2 * w), lambda i, j: (i, j, 0)),
                   pl.BlockSpec((1, HALO_ROWS, 2 * w), lambda i, j: (i, 0, 0))],
        scratch_shapes=[pltpu.VMEM((HALO_ROWS + tm, 2 * w), F32)],
        compiler_params=_cparams("parallel", "arbitrary"),
        name="mixers",
    )(u_mix, halo, wbd, pscale, cw)


def _diff_lambda(lqk_ref, lam_init):
    lq = lqk_ref[...]
    a = jnp.sum(lq[0:1, :] * lq[1:2, :], axis=1, keepdims=True)
    b = jnp.sum(lq[2:3, :] * lq[3:4, :], axis=1, keepdims=True)
    return jnp.exp(a) - jnp.exp(b) + lam_init


def _lane_cumsum(x, steps):
    lane = lax.broadcasted_iota(jnp.int32, x.shape, 1)
    s = 1
    for _ in range(steps):
        x = x + jnp.where(lane >= s, pltpu.roll(x, s, 1), 0.0)
        s *= 2
    return x


def _flash_body(*refs, tq, tk, n_split, fox, lam_init):
    if fox:
        q_ref, k_ref, v_ref, frow_ref, fcol_ref, o_ref, vt_ref, fkc_ref, s0_ref, s1_ref, acc_ref = refs
    else:
        q_ref, k_ref, v_ref, lqk_ref, gcol_ref, o_ref, vt_ref, s0_ref, s1_ref, acc_ref = refs
    qi = pl.program_id(2)
    seq = k_ref.shape[0]
    half = HEAD_DIM
    chunk = min(seq, 512)

    @pl.when(qi == 0)
    def _():
        vt_ref[LANES:, :] = jnp.ones((vt_ref.shape[0] - LANES, seq), BF16)
        for c in range(seq // chunk):
            sl = slice(c * chunk, (c + 1) * chunk)
            vt_ref[0:LANES, sl] = v_ref[sl, :].astype(F32).T.astype(BF16)
            if fox:
                fc = fcol_ref[0, 0, sl, :] * LOG2E
                fkc_ref[0, sl, :] = jnp.broadcast_to(fc[:, 0:1], (chunk, LANES))
                fkc_ref[1, sl, :] = jnp.broadcast_to(fc[:, 1:2], (chunk, LANES))

    qt = q_ref[...].astype(F32).T
    low = lax.broadcasted_iota(jnp.int32, (LANES, tq), 0) < half
    qst = jnp.concatenate([jnp.where(low, qt, 0.0), jnp.where(low, 0.0, qt)], axis=1).astype(BF16)

    acc_ref[...] = jnp.zeros(acc_ref.shape, F32)
    q0 = pl.multiple_of(qi * tq, tq)
    cw = tq // n_split
    n_chain = 2 * n_split
    lanes = [slice(c * cw, (c + 1) * cw) for c in range(n_chain)]
    if fox:
        shift = [frow_ref[0, 0, c // n_split:c // n_split + 1, pl.ds(q0 + (c % n_split) * cw, cw)] * LOG2E
                 for c in range(n_chain)]
    n_rep = cw // LANES

    def chain_mode(c, tile):
        g = c % n_split
        return "full" if (tile is None or g > tile) else ("diag" if g == tile else "skip")

    def scores(ki, s_ref, tile=None):
        k = k_ref[pl.ds(pl.multiple_of(ki * tk, tk), tk), :]
        for c in range(n_chain):
            if chain_mode(c, tile) != "skip":
                s_ref[:, lanes[c]] = _dot(k, qst[:, lanes[c]])

    def softmax_pv(ki, s_ref, ms, tile=None):
        start = pl.multiple_of(ki * tk, tk)
        vt = vt_ref[:, pl.ds(start, tk)]
        new = []
        for c in range(n_chain):
            mode = chain_mode(c, tile)
            if mode == "skip":
                new.append(ms[c])
                continue
            st = s_ref[:, lanes[c]]
            if fox:
                st = st - _rep(fkc_ref[c // n_split, pl.ds(start, tk), :], n_rep)
            if mode == "diag":
                visible = (lax.broadcasted_iota(jnp.int32, (tk, cw), 0)
                           <= lax.broadcasted_iota(jnp.int32, (tk, cw), 1))
                st = jnp.where(visible, st, NEG_INF)
            m_cur = jnp.max(st, axis=0, keepdims=True)
            if fox:
                m_cur = m_cur + shift[c]
            m_new = jnp.maximum(ms[c], m_cur)
            alpha = jnp.exp2(ms[c] - m_new)
            p = jnp.exp2(st - ((m_new - shift[c]) if fox else m_new))
            acc_ref[:, lanes[c]] = alpha * acc_ref[:, lanes[c]] + _dot(vt, p.astype(BF16))
            new.append(m_new)
        return tuple(new)

    def two_tiles(j, ms, diagonal):
        scores(2 * j + 1, s1_ref, 1 if diagonal else None)
        ms = softmax_pv(2 * j, s0_ref, ms, 0 if diagonal else None)
        if not diagonal:
            scores(2 * j + 2, s0_ref)
        return softmax_pv(2 * j + 1, s1_ref, ms, 1 if diagonal else None)

    assert cw == tk and tq == 2 * tk
    scores(0, s0_ref)
    ms = lax.fori_loop(0, qi, lambda j, c: two_tiles(j, c, False), (jnp.full((1, cw), NEG_INF, F32),) * n_chain)
    two_tiles(qi, ms, True)

    acc = acc_ref[...]
    out = acc[0:LANES, :] / acc[LANES:LANES + 1, :]
    if fox:
        res = jnp.where(low, out[:, 0:tq], out[:, tq:])
    else:
        lam = _diff_lambda(lqk_ref, lam_init)
        o = out[:, 0:tq] - lam * out[:, tq:]
        ms = jnp.mean(o * o, axis=0, keepdims=True)
        res = o * lax.rsqrt(ms + RMS_EPS) * gcol_ref[...] * (1.0 - lam_init)
    o_ref[...] = res.T.astype(BF16)


def _flash(q, k, v, extra, *, batch, seq, tq, tk, fox, lam_init=0.0):
    m, w = q.shape
    n_blk = w // LANES
    n_q = seq // tq
    q_spec = pl.BlockSpec((tq, LANES), lambda b, h, i: (b * n_q + i, h))
    kv_spec = pl.BlockSpec((seq, LANES), lambda b, h, i: (b, h))
    assert tq == 2 * tk
    ones_rows = 16
    scratch = [pltpu.VMEM((LANES + ones_rows, seq), BF16)]
    if fox:
        extra_specs = [pl.BlockSpec((1, 1, 2, seq), lambda b, h, i: (b, h, 0, 0)),
                       pl.BlockSpec((1, 1, seq, 2), lambda b, h, i: (h, b, 0, 0))]
        scratch.append(pltpu.VMEM((2, seq, LANES), F32))
    else:
        extra_specs = [_const_spec(extra[0].shape), _const_spec(extra[1].shape)]
    scratch += [pltpu.VMEM((tk, 2 * tq), F32), pltpu.VMEM((tk, 2 * tq), F32),
                pltpu.VMEM((LANES + ones_rows, 2 * tq), F32)]
    return pl.pallas_call(
        functools.partial(_flash_body, tq=tq, tk=tk, n_split=max(1, tq // 256), fox=fox, lam_init=lam_init),
        out_shape=jax.ShapeDtypeStruct((m, w), BF16),
        grid=(batch, n_blk, n_q),
        in_specs=[q_spec, kv_spec, kv_spec] + extra_specs,
        out_specs=q_spec,
        scratch_shapes=scratch,
        compiler_params=_cparams("parallel", "parallel", "arbitrary"),
        name="flash_fox" if fox else "flash_diff",
    )(q, k, v, *extra)


def _fcum_prompt_body(lf_ref, o_ref, *, steps):
    o_ref[...] = _lane_cumsum(lf_ref[...], steps)


def _fcum_prompt(logft, *, batch, seq):
    spec = pl.BlockSpec((8, seq), lambda b: (0, b))
    return pl.pallas_call(
        functools.partial(_fcum_prompt_body, steps=(seq - 1).bit_length()),
        out_shape=jax.ShapeDtypeStruct(logft.shape, F32),
        grid=(batch,),
        in_specs=[spec],
        out_specs=spec,
        compiler_params=_cparams("parallel"),
        name="fcum_prompt",
    )(logft)


def _fcum_past_body(pt_ref, *refs, pages):
    in_refs = refs[:pages]
    o_ref, x_ref, carry_ref = refs[pages:]
    depth, _, n_heads, page = in_refs[0].shape
    grp = depth * n_heads
    g = pl.program_id(1)

    @pl.when(g == 0)
    def _():
        carry_ref[...] = jnp.zeros(carry_ref.shape, F32)

    for j in range(pages):
        for d in range(depth):
            x_ref[pl.ds(j * grp + d * n_heads, n_heads), :] = in_refs[j][d, 0]
    incl = _lane_cumsum(x_ref[...], int(math.log2(page)))
    tot = jnp.broadcast_to(incl[:, page - 1:page], incl.shape)
    row = lax.broadcasted_iota(jnp.int32, incl.shape, 0)
    pre = tot
    s = grp
    while s < pages * grp:
        pre = pre + jnp.where(row >= s, pltpu.roll(pre, s, 0), 0.0)
        s *= 2
    carry = carry_ref[...]
    x_ref[...] = incl + (pre - tot) + jnp.concatenate([carry] * pages, axis=0)
    carry_ref[...] = carry + pre[(pages - 1) * grp:, :]
    for j in range(pages):
        for d in range(depth):
            o_ref[d, 0, j] = x_ref[pl.ds(j * grp + d * n_heads, n_heads), :]


def _fcum_past(logf_t, page_table, *, pages):
    depth, n_pool, n_heads, page = logf_t.shape
    n_seq, n_pages = page_table.shape
    in_specs = [pl.BlockSpec((depth, 1, n_heads, page),
                             functools.partial(lambda b, g, pt, j: (0, pt[b, g * pages + j], 0, 0), j=j))
                for j in range(pages)]
    grid_spec = pltpu.PrefetchScalarGridSpec(
        num_scalar_prefetch=1,
        grid=(n_seq, n_pages // pages),
        in_specs=in_specs,
        out_specs=pl.BlockSpec((depth, 1, pages, n_heads, page), lambda b, g, pt: (0, b, g, 0, 0)),
        scratch_shapes=[pltpu.VMEM((pages * depth * n_heads, page), F32),
                        pltpu.VMEM((depth * n_heads, page), F32)])
    return pl.pallas_call(
        functools.partial(_fcum_past_body, pages=pages),
        out_shape=jax.ShapeDtypeStruct((depth, n_seq, n_pages, n_heads, page), F32),
        grid_spec=grid_spec,
        compiler_params=_cparams("parallel", "arbitrary"),
        name="fcum_past",
    )(page_table, *([logf_t] * pages))


def _softmax_step(s, pv, m_ref, l_ref, acc_ref, shift=None):
    n_k = s.shape[1] // LANES
    n_v = acc_ref.shape[1] // LANES
    m_prev = m_ref[...]
    m_cur = jnp.max(s, axis=1, keepdims=True)
    if shift is not None:
        m_cur = m_cur + shift
    m_new = jnp.maximum(m_prev, m_cur)
    alpha = jnp.exp(m_prev - m_new)
    sub = m_new if shift is None else m_new - shift
    p = jnp.exp(s - _rep(sub, n_k))
    l_ref[...] = alpha * l_ref[...] + jnp.sum(p, axis=1, keepdims=True)
    acc_ref[...] = _rep(alpha, n_v) * acc_ref[...] + pv(p.astype(BF16))
    m_ref[...] = m_new


def _pad_rows(x, rows):
    return jnp.concatenate([x, jnp.zeros((rows - x.shape[0], x.shape[1]), x.dtype)], axis=0)


def _paged_copies(pt_ref, caches, bufs, sems, *, layer, pages, seq, step, slot):
    return [pltpu.make_async_copy(cache.at[layer, pt_ref[seq, step * pages + j]], buf.at[slot, j],
                                  sems.at[a, slot])
            for a, (cache, buf) in enumerate(zip(caches, bufs)) for j in range(pages)]


def _when(cond, fn):
    if isinstance(cond, bool):
        if cond:
            fn()
    else:
        pl.when(cond)(fn)


def _paged_pipeline(pt_ref, caches, bufs, sems, *, layer, pages, seq, step, n_seq, n_steps):
    s = seq * n_steps + step
    slot = lax.rem(s, 2)
    copies = functools.partial(_paged_copies, pt_ref, caches, bufs, sems, layer=layer, pages=pages)

    def start_own():
        for cp in copies(seq=seq, step=step, slot=slot):
            cp.start()

    def start_next():
        wrap = step + 1 == n_steps
        for cp in copies(seq=jnp.where(wrap, seq + 1, seq), step=jnp.where(wrap, 0, step + 1), slot=1 - slot):
            cp.start()

    _when(s == 0, start_own)
    _when(s + 1 < n_seq * n_steps, start_next)
    for cp in copies(seq=seq, step=step, slot=slot):
        cp.wait()
    return slot


def _decode_diff_body(pt_ref, q_ref, knew_ref, vnew_ref, lqk_ref, g_ref, k_hbm, v_hbm, o_ref,
                      kbuf_ref, vbuf_ref, sem_ref, qs_ref, m_ref, l_ref, acc_ref, *, layer, pages, **static):
    g, n_g = pl.program_id(1), pl.num_programs(1)
    slot = _paged_pipeline(pt_ref, (k_hbm, v_hbm), (kbuf_ref, vbuf_ref), sem_ref, layer=layer, pages=pages,
                           seq=pl.program_id(0), step=g, n_seq=pl.num_programs(0), n_steps=n_g)
    _decode_diff_step(q_ref, knew_ref, vnew_ref, lqk_ref, g_ref, o_ref, kbuf_ref, vbuf_ref,
                      qs_ref, m_ref, l_ref, acc_ref, slot=slot, first=g == 0, last=g == n_g - 1,
                      pages=pages, **static)


def _decode_diff_step(q_ref, knew_ref, vnew_ref, lqk_ref, g_ref, o_ref, kbuf_ref, vbuf_ref,
                      qs_ref, m_ref, l_ref, acc_ref, *, slot, first, last, pages, n_heads, t_new, lam_init,
                      between=None):
    page = kbuf_ref.shape[2] // n_heads
    hw = 2 * HEAD_DIM
    rows_h = 2 * t_new

    def init():
        q = q_ref[0]
        low = lax.broadcasted_iota(jnp.int32, (t_new, hw), 1) < HEAD_DIM
        pieces = []
        for h in range(n_heads):
            qh = q[:, h * hw:(h + 1) * hw]
            pieces += [jnp.where(low, qh, 0.0), jnp.where(low, 0.0, qh)]
        qs_ref[...] = jnp.concatenate(pieces, axis=0).astype(BF16)
        m_ref[...] = jnp.full(m_ref.shape, NEG_INF, F32)
        l_ref[...] = jnp.zeros(l_ref.shape, F32)
        acc_ref[...] = jnp.zeros(acc_ref.shape, F32)

    _when(first, init)

    def head_rows(buf_ref, h):
        return jnp.concatenate([buf_ref[slot, j, pl.ds(h, page, stride=n_heads), :].astype(BF16)
                                for j in range(pages)], axis=0)

    def per_head(fn):
        return jnp.concatenate([fn(h, slice(h * rows_h, (h + 1) * rows_h)) for h in range(n_heads)], axis=0)

    qs = qs_ref[...]
    vs = [head_rows(vbuf_ref, h) for h in range(n_heads)]
    s = per_head(lambda h, rows: _dot_nt(qs[rows, :], head_rows(kbuf_ref, h)))
    if between is not None:
        between()
    _softmax_step(s, lambda p: per_head(lambda h, rows: _dot(p[rows, :], vs[h])), m_ref, l_ref, acc_ref)

    def finish():
        k_new = knew_ref[0]
        v_new = vnew_ref[0]
        kn = [_pad_rows(k_new[:, h * hw:(h + 1) * hw], LANES).astype(BF16) for h in range(n_heads)]
        vn = [_pad_rows(v_new[:, h * hw:(h + 1) * hw], LANES).astype(BF16) for h in range(n_heads)]
        s_new = per_head(lambda h, rows: _dot_nt(qs[rows, :], kn[h]))
        row = lax.broadcasted_iota(jnp.int32, s_new.shape, 0) & (t_new - 1)
        col = lax.broadcasted_iota(jnp.int32, s_new.shape, 1)
        s_new = jnp.where(col <= row, s_new, NEG_INF)
        _softmax_step(s_new, lambda p: per_head(lambda h, rows: _dot(p[rows, :], vn[h])), m_ref, l_ref, acc_ref)

        out = acc_ref[...] / l_ref[...]
        lam = _diff_lambda(lqk_ref, lam_init)
        for h in range(n_heads):
            o = out[h * rows_h:h * rows_h + t_new, :] - lam * out[h * rows_h + t_new:(h + 1) * rows_h, :]
            ms = jnp.mean(o * o, axis=-1, keepdims=True)
            o_ref[0, :, h * hw:(h + 1) * hw] = o * lax.rsqrt(ms + RMS_EPS) * g_ref[...] * (1.0 - lam_init)

    _when(last, finish)


def _decode_diff(q, k_new, v_new, cache_k, cache_v, page_table, lqk, g, *, layer, pages, lam_init):
    n_seq, t_new, width = q.shape
    n_heads = width // (2 * HEAD_DIM)
    n_pages = page_table.shape[1]
    rows = cache_k.shape[2]
    seq_spec = pl.BlockSpec((1, t_new, width), lambda b, g_, pt: (b, 0, 0))
    hbm = pl.BlockSpec(memory_space=pl.ANY)
    n_rows = n_heads * 2 * t_new
    grid_spec = pltpu.PrefetchScalarGridSpec(
        num_scalar_prefetch=1,
        grid=(n_seq, n_pages // pages),
        in_specs=[seq_spec, seq_spec, seq_spec,
                  pl.BlockSpec(lqk.shape, lambda b, g_, pt: (0, 0)),
                  pl.BlockSpec(g.shape, lambda b, g_, pt: (0, 0)), hbm, hbm],
        out_specs=seq_spec,
        scratch_shapes=[pltpu.VMEM((2, pages, rows, LANES), F32), pltpu.VMEM((2, pages, rows, LANES), F32),
                        pltpu.SemaphoreType.DMA((2, 2)),
                        pltpu.VMEM((n_rows, LANES), BF16), pltpu.VMEM((n_rows, LANES), F32),
                        pltpu.VMEM((n_rows, LANES), F32), pltpu.VMEM((n_rows, LANES), F32)])
    return pl.pallas_call(
        functools.partial(_decode_diff_body, layer=layer, pages=pages, n_heads=n_heads, t_new=t_new,
                          lam_init=lam_init),
        out_shape=jax.ShapeDtypeStruct((n_seq, t_new, width), F32),
        grid_spec=grid_spec,
        compiler_params=_cparams("arbitrary", "arbitrary"),
        name="decode_diff",
    )(page_table, q, k_new, v_new, lqk, g, cache_k, cache_v)


def _ffn_decode_diff_body(pt_ref, x_ref, wg_ref, wu_ref, wd_ref, lng_ref, lnb_ref,
                          q_ref, knew_ref, vnew_ref, lqk_ref, g_ref, k_hbm, v_hbm, o_ref, yb_ref,
                          acc_ref, kbuf_ref, vbuf_ref, sem_ref, qs_ref, m_ref, l_ref, dacc_ref,
                          *, chunk, alpha, layer, pages, n_steps, **static):
    i = pl.program_id(0)
    x = x_ref[...]
    xb = x.astype(BF16)
    n_chunks = wg_ref.shape[1] // chunk
    per = -(-n_chunks // n_steps)
    for g in range(n_steps):
        slot = _paged_pipeline(pt_ref, (k_hbm, v_hbm), (kbuf_ref, vbuf_ref), sem_ref, layer=layer, pages=pages,
                               seq=i, step=g, n_seq=pl.num_programs(0), n_steps=n_steps)
        mine = list(range(g * per, min((g + 1) * per, n_chunks)))
        ffn = lambda cs: [_ffn_chunk(xb, wg_ref, wu_ref, wd_ref, acc_ref, c, chunk) for c in cs]
        _decode_diff_step(q_ref, knew_ref, vnew_ref, lqk_ref, g_ref, yb_ref, kbuf_ref, vbuf_ref,
                          qs_ref, m_ref, l_ref, dacc_ref, slot=slot, first=g == 0, last=g == n_steps - 1,
                          pages=pages, between=lambda: ffn(mine[:1]), **static)
        ffn(mine[1:])
    z = alpha * x + 0.5 * acc_ref[...]
    o_ref[...] = _layer_norm(z, lng_ref[...], lnb_ref[...])


def _ffn_ln_decode_diff(x, wg, wu, wd, g, b, q, k_new, v_new, cache_k, cache_v, page_table, lqk, sg,
                        *, layer, tm, alpha, pages, lam_init):
    m, d = x.shape
    n_seq, t_new, width = q.shape
    assert m // tm == n_seq
    n_heads = width // (2 * HEAD_DIM)
    n_steps = page_table.shape[1] // pages
    rows = cache_k.shape[2]
    row = pl.BlockSpec((tm, d), lambda i, pt: (i, 0))
    seq_spec = pl.BlockSpec((1, t_new, width), lambda i, pt: (i, 0, 0))
    hbm = pl.BlockSpec(memory_space=pl.ANY)
    n_rows = n_heads * 2 * t_new
    grid_spec = pltpu.PrefetchScalarGridSpec(
        num_scalar_prefetch=1,
        grid=(n_seq,),
        in_specs=[row, _layer_spec(wg, layer), _layer_spec(wu, layer), _layer_spec(wd, layer),
                  _const_spec((1, d)), _const_spec((1, d)), seq_spec, seq_spec, seq_spec,
                  _const_spec(lqk.shape), _const_spec(sg.shape), hbm, hbm],
        out_specs=[row, seq_spec],
        scratch_shapes=[pltpu.VMEM((tm, d), F32),
                        pltpu.VMEM((2, pages, rows, LANES), F32), pltpu.VMEM((2, pages, rows, LANES), F32),
                        pltpu.SemaphoreType.DMA((2, 2)),
                        pltpu.VMEM((n_rows, LANES), BF16), pltpu.VMEM((n_rows, LANES), F32),
                        pltpu.VMEM((n_rows, LANES), F32), pltpu.VMEM((n_rows, LANES), F32)])
    return pl.pallas_call(
        functools.partial(_ffn_decode_diff_body, chunk=256, alpha=alpha, layer=layer, pages=pages,
                          n_steps=n_steps, n_heads=n_heads, t_new=t_new, lam_init=lam_init),
        out_shape=[jax.ShapeDtypeStruct((m, d), F32), jax.ShapeDtypeStruct((n_seq, t_new, width), F32)],
        grid_spec=grid_spec,
        compiler_params=_cparams("arbitrary"),
        name="ffn_ln_decode_diff",
    )(page_table, x, wg, wu, wd, g, b, q, k_new, v_new, lqk, sg, cache_k, cache_v)


def _decode_fox_body(pt_ref, q_ref, knew_ref, vnew_ref, lfnew_ref, fk_ref, kt_hbm, vt_hbm, o_ref,
                     ktbuf_ref, vtbuf_ref, sem_ref, qs_ref, m_ref, l_ref, acc_ref, *, layer, pages, **static):
    g, n_g = pl.program_id(1), pl.num_programs(1)
    slot = _paged_pipeline(pt_ref, (kt_hbm, vt_hbm), (ktbuf_ref, vtbuf_ref), sem_ref, layer=layer, pages=pages,
                           seq=pl.program_id(0), step=g, n_seq=pl.num_programs(0), n_steps=n_g)
    _decode_fox_step(q_ref, knew_ref, vnew_ref, lfnew_ref, fk_ref, o_ref, ktbuf_ref, vtbuf_ref,
                     qs_ref, m_ref, l_ref, acc_ref, slot=slot, first=g == 0, last=g == n_g - 1,
                     page_base=g * pages, pages=pages, **static)


def _decode_fox_step(q_ref, knew_ref, vnew_ref, lfnew_ref, fk_ref, o_ref, ktbuf_ref, vtbuf_ref,
                     qs_ref, m_ref, l_ref, acc_ref, *, slot, first, last, page_base, pages, n_heads, t_new,
                     between=None):
    width = q_ref.shape[2]
    page = ktbuf_ref.shape[3]
    n_rows = n_heads * t_new

    def init():
        q = q_ref[0]
        lane = lax.broadcasted_iota(jnp.int32, (t_new, width), 1)
        qs_ref[...] = jnp.concatenate(
            [jnp.where((lane >= h * HEAD_DIM) & (lane < (h + 1) * HEAD_DIM), q, 0.0) for h in range(n_heads)],
            axis=0).astype(BF16)
        m_ref[...] = jnp.full(m_ref.shape, NEG_INF, F32)
        l_ref[...] = jnp.zeros(l_ref.shape, F32)
        acc_ref[...] = jnp.zeros(acc_ref.shape, F32)

    _when(first, init)

    def head_rows(per_head):
        return jnp.concatenate([jnp.broadcast_to(per_head(h), (t_new, per_head(h).shape[1]))
                                for h in range(n_heads)], axis=0)

    total = fk_ref[0, 0, fk_ref.shape[2] - 1, :, page - 1:page]
    f_new = _lane_cumsum(lfnew_ref[0], int(math.log2(t_new))) + jnp.concatenate(
        [total, jnp.zeros((8 - n_heads, 1), F32)], axis=0)
    eye = (lax.broadcasted_iota(jnp.int32, (t_new, LANES), 0)
           == lax.broadcasted_iota(jnp.int32, (t_new, LANES), 1))
    fq_col = jnp.concatenate(
        [jnp.sum(jnp.where(eye, jnp.broadcast_to(f_new[h:h + 1, :], (t_new, LANES)), 0.0),
                 axis=1, keepdims=True) for h in range(n_heads)], axis=0)
    shift = jnp.broadcast_to(fq_col, (n_rows, LANES))

    n_chain = m_ref.shape[0]
    per = pages // n_chain
    qs = qs_ref[...]
    scores = []
    for c in range(n_chain):
        sel = range(c * per, (c + 1) * per)
        kt = jnp.concatenate([ktbuf_ref[slot, j].astype(BF16) for j in sel], axis=1)
        fk = jnp.concatenate([head_rows(lambda h: fk_ref[0, 0, page_base + j, h:h + 1, :]) for j in sel], axis=1)
        scores.append(_dot(qs, kt) - fk)
    if between is not None:
        between()
    for c in range(n_chain):
        vt = jnp.concatenate([vtbuf_ref[slot, j].astype(BF16) for j in range(c * per, (c + 1) * per)], axis=1)
        _softmax_step(scores[c], lambda p, vt=vt: _dot_nt(p, vt), m_ref.at[c], l_ref.at[c], acc_ref.at[c], shift)

    def finish():
        k_new = _pad_rows(knew_ref[0], LANES).astype(BF16)
        v_new = _pad_rows(vnew_ref[0], LANES).astype(BF16)
        s_new = _dot_nt(qs, k_new) - head_rows(lambda h: f_new[h:h + 1, :])
        row = lax.broadcasted_iota(jnp.int32, s_new.shape, 0) & (t_new - 1)
        col = lax.broadcasted_iota(jnp.int32, s_new.shape, 1)
        s_new = jnp.where(col <= row, s_new, NEG_INF)
        _softmax_step(s_new, lambda p: _dot(p, v_new), m_ref.at[0], l_ref.at[0], acc_ref.at[0], shift)

        m_all = m_ref[0]
        for c in range(1, n_chain):
            m_all = jnp.maximum(m_all, m_ref[c])
        l_all = jnp.zeros(m_all.shape, F32)
        acc_all = jnp.zeros(acc_ref.shape[1:], F32)
        for c in range(n_chain):
            wgt = jnp.exp(m_ref[c] - m_all)
            l_all = l_all + wgt * l_ref[c]
            acc_all = acc_all + _rep(wgt, width // LANES) * acc_ref[c]
        out = acc_all / _rep(l_all, width // LANES)
        lane = lax.broadcasted_iota(jnp.int32, (t_new, width), 1)
        res = jnp.zeros((t_new, width), F32)
        for h in range(n_heads):
            keep = (lane >= h * HEAD_DIM) & (lane < (h + 1) * HEAD_DIM)
            res = jnp.where(keep, out[h * t_new:(h + 1) * t_new, :], res)
        o_ref[0] = res

    _when(last, finish)


def _decode_fox_pieces(q, cache_kt, page_table, fk, *, layer, pages, seq_index):
    n_seq, t_new, width = q.shape
    n_heads = width // HEAD_DIM
    page = cache_kt.shape[3]
    n_pages = page_table.shape[1]
    seq_spec = pl.BlockSpec((1, t_new, width), lambda *a: (seq_index(*a), 0, 0))
    hbm = pl.BlockSpec(memory_space=pl.ANY)
    n_rows = n_heads * t_new
    n_chain = 2 if pages % 2 == 0 else 1
    in_specs = [seq_spec, seq_spec, seq_spec,
                pl.BlockSpec((1, 8, LANES), lambda *a: (seq_index(*a), 0, 0)),
                pl.BlockSpec((1, 1, n_pages, n_heads, page), lambda *a: (layer, seq_index(*a), 0, 0, 0)),
                hbm, hbm]
    scratch = [pltpu.VMEM((2, pages, width, page), F32), pltpu.VMEM((2, pages, width, page), F32),
               pltpu.SemaphoreType.DMA((2, 2)),
               pltpu.VMEM((n_rows, width), BF16), pltpu.VMEM((n_chain, n_rows, LANES), F32),
               pltpu.VMEM((n_chain, n_rows, LANES), F32), pltpu.VMEM((n_chain, n_rows, width), F32)]
    return in_specs, seq_spec, scratch, dict(n_heads=n_heads, t_new=t_new)


def _decode_fox(q, k_new, v_new, cache_kt, cache_vt, page_table, lf_new, fk, *, layer, pages):
    n_seq, t_new, width = q.shape
    in_specs, seq_spec, scratch, static = _decode_fox_pieces(
        q, cache_kt, page_table, fk, layer=layer, pages=pages, seq_index=lambda b, g, pt: b)
    grid_spec = pltpu.PrefetchScalarGridSpec(
        num_scalar_prefetch=1, grid=(n_seq, page_table.shape[1] // pages),
        in_specs=in_specs, out_specs=seq_spec, scratch_shapes=scratch)
    return pl.pallas_call(
        functools.partial(_decode_fox_body, layer=layer, pages=pages, **static),
        out_shape=jax.ShapeDtypeStruct((n_seq, t_new, width), F32),
        grid_spec=grid_spec,
        compiler_params=_cparams("arbitrary", "arbitrary"),
        name="decode_fox",
    )(page_table, q, k_new, v_new, lf_new, fk, cache_kt, cache_vt)


def _ffn_decode_fox_body(pt_ref, x_ref, wg_ref, wu_ref, wd_ref, lng_ref, lnb_ref,
                         q_ref, knew_ref, vnew_ref, lfnew_ref, fk_ref, kt_hbm, vt_hbm, o_ref, yc_ref,
                         acc_ref, ktbuf_ref, vtbuf_ref, sem_ref, qs_ref, m_ref, l_ref, dacc_ref,
                         *, chunk, alpha, layer, pages, n_steps, **static):
    i = pl.program_id(0)
    x = x_ref[...]
    xb = x.astype(BF16)
    n_chunks = wg_ref.shape[1] // chunk
    per = -(-n_chunks // n_steps)
    for g in range(n_steps):
        slot = _paged_pipeline(pt_ref, (kt_hbm, vt_hbm), (ktbuf_ref, vtbuf_ref), sem_ref, layer=layer,
                               pages=pages, seq=i, step=g, n_seq=pl.num_programs(0), n_steps=n_steps)
        mine = list(range(g * per, min((g + 1) * per, n_chunks)))
        ffn = lambda cs: [_ffn_chunk(xb, wg_ref, wu_ref, wd_ref, acc_ref, c, chunk) for c in cs]
        _decode_fox_step(q_ref, knew_ref, vnew_ref, lfnew_ref, fk_ref, yc_ref, ktbuf_ref, vtbuf_ref,
                         qs_ref, m_ref, l_ref, dacc_ref, slot=slot, first=g == 0, last=g == n_steps - 1,
                         page_base=g * pages, pages=pages, between=lambda: ffn(mine[:1]), **static)
        ffn(mine[1:])
    z = alpha * x + 0.5 * acc_ref[...]
    o_ref[...] = _layer_norm(z, lng_ref[...], lnb_ref[...])


def _ffn_ln_decode_fox(x, wg, wu, wd, g, b, q, k_new, v_new, cache_kt, cache_vt, page_table, lf_new, fk,
                       *, layer, tm, alpha, pages):
    m, d = x.shape
    n_seq, t_new, width = q.shape
    assert m // tm == n_seq
    dec_specs, seq_spec, dec_scratch, static = _decode_fox_pieces(
        q, cache_kt, page_table, fk, layer=layer, pages=pages, seq_index=lambda i, pt: i)
    row = pl.BlockSpec((tm, d), lambda i, pt: (i, 0))
    grid_spec = pltpu.PrefetchScalarGridSpec(
        num_scalar_prefetch=1,
        grid=(n_seq,),
        in_specs=[row, _layer_spec(wg, layer), _layer_spec(wu, layer), _layer_spec(wd, layer),
                  _const_spec((1, d)), _const_spec((1, d))] + dec_specs,
        out_specs=[row, seq_spec],
        scratch_shapes=[pltpu.VMEM((tm, d), F32)] + dec_scratch)
    return pl.pallas_call(
        functools.partial(_ffn_decode_fox_body, chunk=256, alpha=alpha, layer=layer, pages=pages,
                          n_steps=page_table.shape[1] // pages, **static),
        out_shape=[jax.ShapeDtypeStruct((m, d), F32), jax.ShapeDtypeStruct((n_seq, t_new, width), F32)],
        grid_spec=grid_spec,
        compiler_params=_cparams("arbitrary"),
        name="ffn_ln_decode_fox",
    )(page_table, x, wg, wu, wd, g, b, q, k_new, v_new, lf_new, fk, cache_kt, cache_vt)


def _merge_body(x_ref, yad_ref, yb_ref, yc_ref, wg_ref, wp_ref, wo_ref, g_ref, b_ref, o_ref, *, alpha):
    x = x_ref[...]
    xb = x.astype(BF16)
    d = x.shape[1]
    w = d // 4
    yad = yad_ref[...]
    ys = (yad[:, 0:w], yb_ref[...], yc_ref[...], yad[:, w:2 * w])
    merged = None
    row = 0
    for i, y in enumerate(ys):
        gate = jax.nn.sigmoid(_dot(xb, wg_ref[:, i * d:(i + 1) * d]))
        term = gate * _dot(y.astype(BF16), wp_ref[row:row + y.shape[1], :])
        merged = term if merged is None else merged + term
        row += y.shape[1]
    out = _dot(merged.astype(BF16), wo_ref[...])
    o_ref[...] = _layer_norm(alpha * x + out, g_ref[...], b_ref[...])


def _merge(x, yad, yb, yc, wg, wp, wo, g, b, *, layer, tm, alpha):
    m, d = x.shape
    row = lambda w: pl.BlockSpec((tm, w), lambda i: (i, 0))
    return pl.pallas_call(
        functools.partial(_merge_body, alpha=alpha),
        out_shape=jax.ShapeDtypeStruct((m, d), F32),
        grid=(m // tm,),
        in_specs=[row(d), row(yad.shape[1]), row(yb.shape[1]), row(yc.shape[1]),
                  _layer_spec(wg, layer), _layer_spec(wp, layer), _layer_spec(wo, layer),
                  _const_spec((1, d)), _const_spec((1, d))],
        out_specs=row(d),
        compiler_params=_cparams("parallel"),
        name="merge",
    )(x, yad, yb, yc, wg, wp, wo, g, b)


def _rope_tables(pos):
    half = HEAD_DIM // 2
    inv = ROPE_THETA ** (-jnp.arange(half, dtype=F32) / half)
    ang = pos.astype(F32)[:, None] * inv[None, :]
    cos, sin = jnp.cos(ang), jnp.sin(ang)
    cos_t = jnp.tile(cos, (1, LANES // half))
    sin_t = jnp.tile(jnp.concatenate([-sin, sin], axis=1), (1, LANES // HEAD_DIM))
    return cos_t, sin_t


def _pick_tile(n, target):
    t = min(n, target)
    while n % t:
        t //= 2
    return t


def kernel(x_prompt, x_sample, cache_diff_k, cache_diff_v, cache_fox_k, cache_fox_v, cache_fox_logf,
           state_pool, state_conv, page_table, w_in, b_fgate, pool_w, pool_scale, lambda_qk, subln_g,
           conv_w, w_branch, w_o, w_ffn1_gate, w_ffn1_up, w_ffn1_down, w_ffn2_gate, w_ffn2_up,
           w_ffn2_down, ln_g, ln_b):
    batch, seq, d = x_prompt.shape
    n_seq, t_new, _ = x_sample.shape
    depth = w_in.shape[0]
    n_pool, page = cache_diff_k.shape[1:3]
    past_len = page_table.shape[1] * page
    h_diff = cache_diff_k.shape[3]
    h_fox = cache_fox_k.shape[3]
    w_pool = state_pool.shape[-1]
    w_diff, w_fox, w_conv = d // 2, d // 4, d // 4
    alpha = (2.0 * depth) ** 0.25

    sizes = (w_pool, w_diff, w_diff, w_diff, w_fox, w_fox, w_fox, h_fox, w_conv, w_conv, w_conv, 4 * d)
    offs = np.concatenate([[0], np.cumsum(sizes)])
    col = lambda i: w_in[:, :, offs[i]:offs[i + 1]]
    w_main = jnp.concatenate([col(0), col(8), col(9), col(10), col(1), col(2), col(3), col(4), col(5), col(6)],
                             axis=-1).astype(BF16)
    w_ft = jnp.pad(jnp.swapaxes(col(7), 1, 2), ((0, 0), (0, 16 - h_fox), (0, 0))).astype(BF16)
    b_f = jnp.pad(b_fgate, ((0, 0), (0, 8 - h_fox)))[:, :, None]
    w_gate = col(11).astype(BF16)
    w_br = w_branch.astype(BF16)
    w_out = w_o.astype(BF16)
    ffn = [tuple(w.astype(BF16) for w in ws) for ws in
           ((w_ffn1_gate, w_ffn1_up, w_ffn1_down), (w_ffn2_gate, w_ffn2_up, w_ffn2_down))]
    n_grp, grp = pool_w.shape[1], pool_w.shape[2]
    wbd = jnp.zeros((depth, w_pool, w_pool), F32)
    for gidx in range(n_grp):
        wbd = wbd.at[:, gidx * grp:(gidx + 1) * grp, gidx * grp:(gidx + 1) * grp].set(pool_w[:, gidx])
    wbd = wbd.astype(BF16)
    pscale = pool_scale[:, None, :]

    cos_p, sin_p = _rope_tables(jnp.arange(seq, dtype=jnp.int32))
    cos_s, sin_s = _rope_tables(past_len + jnp.arange(t_new, dtype=jnp.int32))
    cos_s, sin_s = jnp.tile(cos_s, (n_seq, 1)), jnp.tile(sin_s, (n_seq, 1))

    ck_diff = cache_diff_k.reshape(depth, n_pool, page * h_diff, 2 * HEAD_DIM)
    cv_diff = cache_diff_v.reshape(depth, n_pool, page * h_diff, 2 * HEAD_DIM)
    ck_fox = jnp.transpose(cache_fox_k, (0, 1, 3, 4, 2)).reshape(depth, n_pool, h_fox * HEAD_DIM, page)
    cv_fox = jnp.transpose(cache_fox_v, (0, 1, 3, 4, 2)).reshape(depth, n_pool, h_fox * HEAD_DIM, page)
    pages = _pick_tile(page_table.shape[1], 16)
    fk_past = _fcum_past(jnp.transpose(cache_fox_logf, (0, 1, 3, 2)), page_table, pages=pages)

    halo_p = jnp.zeros((batch, HALO_ROWS, 2 * w_pool), F32)

    tm_p = _pick_tile(batch * seq, 512)
    tm_s = n_seq * t_new
    tq = _pick_tile(seq, 512)
    tk = tq // 2

    pages_fox = _pick_tile(page_table.shape[1], 32)
    p_kv = [jnp.zeros((depth, batch * seq * h_diff, 2 * HEAD_DIM), F32) for _ in range(2)]
    p_kv += [jnp.zeros((depth, batch, h_fox * HEAD_DIM, seq), F32) for _ in range(2)]

    xp = x_prompt.reshape(batch * seq, d)
    xs = x_sample.reshape(n_seq * t_new, d)
    p_rows, s_rows = [], []
    for l in range(depth):
        lam_init = 0.8 - 0.6 * math.exp(-0.3 * l)
        lng = lambda i: ln_g[l, i][None, :]
        lnb = lambda i: ln_b[l, i][None, :]
        halo_s = jnp.concatenate(
            [jnp.pad(state_pool[l], ((0, 0), (HALO_ROWS - state_pool.shape[2], 0), (0, 0))),
             jnp.pad(state_conv[l], ((0, 0), (HALO_ROWS - state_conv.shape[2], 0), (0, 0)))], axis=-1)

        xs = _ffn_ln(xs, *ffn[0], lng(0), lnb(0), layer=l, tm=tm_s, alpha=alpha)
        u_mix, s_qd, _, _, s_qf, _, _, s_logft, s_kd, s_vd, s_kf, s_vf = _in_proj(
            xs, w_main, w_ft, b_f, cos_s, sin_s, layer=l, tm=tm_s, q_scale=1.0 / math.sqrt(HEAD_DIM))
        s_yad, s_state = _mixers(u_mix.reshape(n_seq, t_new, d), halo_s, wbd, pscale, conv_w,
                                 layer=l, tm=t_new, pos0=past_len)
        seq3 = lambda a: a.reshape(n_seq, t_new, a.shape[-1])
        diff_args = (seq3(s_qd.astype(F32)), seq3(s_kd), seq3(s_vd), ck_diff, cv_diff, page_table,
                     lambda_qk[l], subln_g[l][None, :])

        if (batch * seq) // tm_p == n_seq:
            xp, s_yb = _ffn_ln_decode_diff(xp, *ffn[0], lng(0), lnb(0), *diff_args, layer=l, tm=tm_p,
                                           alpha=alpha, pages=pages, lam_init=lam_init)
        else:
            xp = _ffn_ln(xp, *ffn[0], lng(0), lnb(0), layer=l, tm=tm_p, alpha=alpha)
            s_yb = _decode_diff(*diff_args, layer=l, pages=pages, lam_init=lam_init)
        yad, state, qd, kd16, vd16, qf, kf16, vf16, logft, *p_kv = _in_proj(
            xp, w_main, w_ft, b_f, cos_p, sin_p, layer=l, tm=tm_p,
            q_scale=LOG2E / math.sqrt(HEAD_DIM),
            stacked=(halo_p, wbd, pscale, conv_w, *p_kv))
        yb = _flash(qd, kd16, vd16, (lambda_qk[l], subln_g[l][:, None]), batch=batch, seq=seq, tq=tq, tk=tk,
                    fox=False, lam_init=lam_init)
        logf = logft[:h_fox].reshape(h_fox, batch, seq)
        f_cum = _fcum_prompt(logft, batch=batch, seq=seq)[:h_fox].reshape(h_fox // 2, 2, batch, seq)
        f_row = jnp.transpose(f_cum, (2, 0, 1, 3))
        f_col = jnp.transpose(f_cum, (0, 2, 3, 1))
        yc = _flash(qf, kf16, vf16, (f_row, f_col), batch=batch, seq=seq, tq=tq, tk=tk, fox=True)
        xp = _merge(xp, yad, yb, yc, w_gate, w_br, w_out, lng(1), lnb(1), layer=l, tm=tm_p, alpha=alpha)
        s_logf = s_logft[:h_fox].reshape(h_fox, n_seq, t_new)
        lf_new = jnp.pad(jnp.transpose(s_logf, (1, 0, 2)), ((0, 0), (0, 8 - h_fox), (0, LANES - t_new)))
        fox_args = (seq3(s_qf.astype(F32)), seq3(s_kf), seq3(s_vf), ck_fox, cv_fox, page_table, lf_new, fk_past)
        if (batch * seq) // tm_p == n_seq:
            xp, s_yc = _ffn_ln_decode_fox(xp, *ffn[1], lng(2), lnb(2), *fox_args, layer=l, tm=tm_p,
                                          alpha=alpha, pages=pages_fox)
        else:
            xp = _ffn_ln(xp, *ffn[1], lng(2), lnb(2), layer=l, tm=tm_p, alpha=alpha)
            s_yc = _decode_fox(*fox_args, layer=l, pages=pages_fox)
        p_rows.append((jnp.transpose(logf, (1, 2, 0)),
                       state[:, 1:, :w_pool], state[:, HALO_ROWS - state_conv.shape[2]:, w_pool:]))

        xs = _merge(xs, s_yad.reshape(n_seq * t_new, 2 * w_pool), s_yb.reshape(n_seq * t_new, -1),
                    s_yc.reshape(n_seq * t_new, -1), w_gate, w_br, w_out, lng(1), lnb(1), layer=l, tm=tm_s,
                    alpha=alpha)
        xs = _ffn_ln(xs, *ffn[1], lng(2), lnb(2), layer=l, tm=tm_s, alpha=alpha)
        s_rows.append((s_kd.reshape(n_seq, t_new, h_diff, -1), s_vd.reshape(n_seq, t_new, h_diff, -1),
                       s_kf.reshape(n_seq, t_new, h_fox, -1), s_vf.reshape(n_seq, t_new, h_fox, -1),
                       jnp.transpose(s_logf, (1, 2, 0)),
                       s_state[:, 1:, :w_pool], s_state[:, HALO_ROWS - state_conv.shape[2]:, w_pool:]))

    p_small = [jnp.stack([r[i] for r in p_rows], axis=0) for i in range(3)]
    s_out = [jnp.stack([r[i] for r in s_rows], axis=0) for i in range(7)]
    kd_all, vd_all, kf_all, vf_all = p_kv
    fox_rows = lambda a: jnp.transpose(a.reshape(depth, batch, h_fox, HEAD_DIM, seq), (0, 1, 4, 2, 3))
    return (xp.reshape(batch, seq, d), xs.reshape(n_seq, t_new, d),
            kd_all.reshape(depth, batch, seq, h_diff, 2 * HEAD_DIM),
            vd_all.reshape(depth, batch, seq, h_diff, 2 * HEAD_DIM),
            fox_rows(kf_all), fox_rows(vf_all), *p_small, *s_out)
```

```python
import functools
import math

import jax
import jax.numpy as jnp
import numpy as np
from jax import lax
from jax.experimental import pallas as pl
from jax.experimental.pallas import tpu as pltpu

F32 = jnp.float32
BF16 = jnp.bfloat16

HEAD_DIM = 64
ROPE_THETA = 10000.0
LN_EPS = 1e-5
RMS_EPS = 1e-5
NEG_INF = -1e30
LOG2E = math.log2(math.e)
POOL_WINDOWS = (2, 4, 8, 16)
HALO_ROWS = 16
LANES = 128
FFN_CHUNK = 256
VMEM_LIMIT_BYTES = 56 * 1024 * 1024


def _cparams(*sem):
    return pltpu.CompilerParams(dimension_semantics=sem, vmem_limit_bytes=VMEM_LIMIT_BYTES)


def _const_spec(shape):
    zeros = (0,) * len(shape)
    return pl.BlockSpec(shape, lambda *_: zeros, pipeline_mode=pl.Buffered(1))


def _layer_spec(arr, layer):
    zeros = (0,) * (arr.ndim - 1)
    return pl.BlockSpec((None,) + arr.shape[1:], lambda *_: (layer,) + zeros, pipeline_mode=pl.Buffered(1))


def _dot(a, b):
    return jnp.dot(a, b, preferred_element_type=F32)


def _dot_nt(a, b):
    return lax.dot_general(a, b, (((1,), (1,)), ((), ())), preferred_element_type=F32)


def _rep(x, n):
    return x if n == 1 else jnp.concatenate([x] * n, axis=1)


def _layer_norm(z, g, b):
    mu = jnp.mean(z, axis=-1, keepdims=True)
    zc = z - mu
    var = jnp.mean(zc * zc, axis=-1, keepdims=True)
    return zc * lax.rsqrt(var + LN_EPS) * g + b


def _ffn_chunk(xb, wg_ref, wu_ref, wd_ref, acc_ref, c, chunk):
    sl = slice(c * chunk, (c + 1) * chunk)
    gate = _dot(xb, wg_ref[:, sl])
    up = _dot(xb, wu_ref[:, sl])
    h = gate * jax.nn.sigmoid(gate) * up
    d = _dot(h.astype(BF16), wd_ref[sl, :])
    if c == 0:
        acc_ref[...] = d
    else:
        acc_ref[...] += d


def _ffn_ln_body(x_ref, wg_ref, wu_ref, wd_ref, g_ref, b_ref, o_ref, acc_ref, *, chunk, alpha):
    x = x_ref[...]
    xb = x.astype(BF16)
    for c in range(wg_ref.shape[1] // chunk):
        _ffn_chunk(xb, wg_ref, wu_ref, wd_ref, acc_ref, c, chunk)
    z = alpha * x + 0.5 * acc_ref[...]
    o_ref[...] = _layer_norm(z, g_ref[...], b_ref[...])


def _ffn_ln(x, wg, wu, wd, g, b, *, layer, tm, alpha):
    m, d = x.shape
    row = pl.BlockSpec((tm, d), lambda i: (i, 0))
    return pl.pallas_call(
        functools.partial(_ffn_ln_body, chunk=FFN_CHUNK, alpha=alpha),
        out_shape=jax.ShapeDtypeStruct((m, d), F32),
        grid=(m // tm,),
        in_specs=[row, _layer_spec(wg, layer), _layer_spec(wu, layer), _layer_spec(wd, layer),
                  _const_spec((1, d)), _const_spec((1, d))],
        out_specs=row,
        scratch_shapes=[pltpu.VMEM((tm, d), F32)],
        compiler_params=_cparams("parallel"),
        name="ffn_ln",
    )(x, wg, wu, wd, g, b)


def _mix_tile(u, ext_ref, wbd, pscale, cw, pos_start):
    tm = u.shape[0]
    w = u.shape[1] // 4
    up = u[:, 0:w]
    cb = u[:, w:2 * w]
    ext_ref[HALO_ROWS:HALO_ROWS + tm, 0:w] = up
    ext_ref[HALO_ROWS:HALO_ROWS + tm, w:2 * w] = u[:, 2 * w:3 * w] * u[:, 3 * w:4 * w]
    e = ext_ref[...]
    ep = e[:, 0:w]
    ev = e[:, w:2 * w]

    sums = []
    b = ep
    for k in (1, 2, 4, 8):
        b = b + pltpu.roll(b, k, 0)
        sums.append(b[HALO_ROWS:, :])
    lane = lax.broadcasted_iota(jnp.int32, (tm, w), 1)
    grp = w // len(POOL_WINDOWS)
    win_sum = jnp.where(lane < grp, sums[0],
                        jnp.where(lane < 2 * grp, sums[1], jnp.where(lane < 3 * grp, sums[2], sums[3])))
    window = jnp.where(lane < grp, POOL_WINDOWS[0],
                       jnp.where(lane < 2 * grp, POOL_WINDOWS[1],
                                 jnp.where(lane < 3 * grp, POOL_WINDOWS[2], POOL_WINDOWS[3])))
    pos = pos_start + lax.broadcasted_iota(jnp.int32, (tm, w), 0)
    cnt = jnp.minimum(pos + 1, window).astype(F32)
    pooled = win_sum / cnt - up
    ya = _dot(pooled.astype(BF16), wbd) * pscale

    conv = cw[0:1, :] * pltpu.roll(ev, 2, 0) + cw[1:2, :] * pltpu.roll(ev, 1, 0) + cw[2:3, :] * ev
    yd = cb * conv[HALO_ROWS:, :]
    return ya, yd, e[tm:tm + HALO_ROWS, :]


def _in_proj_body(*refs, d, q_scale, stacked, n_tab):
    if stacked:
        (x_ref, w_ref, wft_ref, bf_ref, cos_ref, sin_ref, halo_ref, wbd_ref, pscale_ref, cw_ref,
         _, _, _, _,
         yad_ref, state_ref, qd_ref, kd16_ref, vd16_ref, qf_ref, kf16_ref, vf16_ref, logft_ref,
         kd32_ref, vd32_ref, kf32_ref, vf32_ref, ext_ref) = refs
    else:
        (x_ref, w_ref, wft_ref, bf_ref, cos_ref, sin_ref,
         umix_ref, qd_ref, kd16_ref, vd16_ref, qf_ref, kf16_ref, vf16_ref, logft_ref,
         kd32_ref, vd32_ref, kf32_ref, vf32_ref) = refs
    w_mix, w_diff, w_fox = d, d // 2, d // 4
    n_blk = w_diff // LANES
    xb = x_ref[...].astype(BF16)
    tm = xb.shape[0]

    def put_diff(ref, j, val):
        if stacked:
            ref[0, pl.ds(j, tm, stride=n_blk), :] = val
        else:
            ref[:, j * LANES:(j + 1) * LANES] = val

    def put_fox(ref, val):
        if stacked:
            ref[0, 0] = val.T
        else:
            ref[...] = val

    u_mix = _dot(xb, w_ref[:, 0:w_mix])
    if stacked:
        ti = pl.program_id(0) % n_tab

        @pl.when(ti == 0)
        def _():
            ext_ref[0:HALO_ROWS, :] = halo_ref[0]

        ya, yd, new_halo = _mix_tile(u_mix, ext_ref, wbd_ref[...], pscale_ref[...], cw_ref[...], ti * tm)
        yad_ref[:, 0:w_fox] = ya.astype(BF16)
        yad_ref[:, w_fox:2 * w_fox] = yd.astype(BF16)
        ext_ref[0:HALO_ROWS, :] = new_halo
        state_ref[0] = new_halo
    else:
        umix_ref[...] = u_mix

    qk = _dot(xb, w_ref[:, w_mix:w_mix + 2 * w_diff])
    cos = cos_ref[...]
    sin = sin_ref[...]
    lane = lax.broadcasted_iota(jnp.int32, (tm, LANES), 1)
    first_half = (lane & (HEAD_DIM // 2)) == 0
    for j in range(2 * n_blk):
        blk = qk[:, j * LANES:(j + 1) * LANES]
        partner = jnp.where(first_half, pltpu.roll(blk, LANES - HEAD_DIM // 2, 1),
                            pltpu.roll(blk, HEAD_DIM // 2, 1))
        r = blk * cos + partner * sin
        if j < n_blk:
            qd_ref[:, j * LANES:(j + 1) * LANES] = (r * q_scale).astype(BF16)
        else:
            put_diff(kd32_ref, j - n_blk, r)
            kd16_ref[:, (j - n_blk) * LANES:(j - n_blk + 1) * LANES] = r.astype(BF16)

    off = w_mix + 2 * w_diff
    vd = _dot(xb, w_ref[:, off:off + w_diff])
    for j in range(n_blk):
        put_diff(vd32_ref, j, vd[:, j * LANES:(j + 1) * LANES])
    vd16_ref[...] = vd.astype(BF16)

    off += w_diff
    fox = _dot(xb, w_ref[:, off:off + 3 * w_fox])
    qf_ref[...] = (fox[:, 0:w_fox] * q_scale).astype(BF16)
    kf = fox[:, w_fox:2 * w_fox]
    put_fox(kf32_ref, kf)
    kf16_ref[...] = kf.astype(BF16)
    vf = fox[:, 2 * w_fox:3 * w_fox]
    put_fox(vf32_ref, vf)
    vf16_ref[...] = vf.astype(BF16)

    ft = _dot_nt(wft_ref[...], xb)
    logft_ref[...] = jax.nn.log_sigmoid(ft[0:8, :] + bf_ref[...])


def _in_proj(x, w_main, w_ft, b_f, cos_t, sin_t, *, layer, tm, q_scale, stacked=None):
    m, d = x.shape
    n_tab = cos_t.shape[0] // tm
    w_diff, w_fox = d // 2, d // 4
    n_blk = w_diff // LANES
    row = lambda w: pl.BlockSpec((tm, w), lambda i: (i, 0))
    tab = pl.BlockSpec((tm, LANES), lambda i: (i % n_tab, 0))
    sds = jax.ShapeDtypeStruct
    out_shape = [sds((m, w_diff), BF16), sds((m, w_diff), BF16), sds((m, w_diff), BF16),
                 sds((m, w_fox), BF16), sds((m, w_fox), BF16), sds((m, w_fox), BF16), sds((8, m), F32)]
    out_specs = [row(w_diff), row(w_diff), row(w_diff), row(w_fox), row(w_fox), row(w_fox),
                 pl.BlockSpec((8, tm), lambda i: (0, i))]
    in_specs = [row(d), _layer_spec(w_main, layer), _layer_spec(w_ft, layer), _layer_spec(b_f, layer), tab, tab]
    args = [x, w_main, w_ft, b_f, cos_t, sin_t]
    aliases = {}
    scratch = []
    if stacked is None:
        out_shape = [sds((m, d), F32)] + out_shape + [sds((m, w_diff), F32), sds((m, w_diff), F32),
                                                      sds((m, w_fox), F32), sds((m, w_fox), F32)]
        out_specs = [row(d)] + out_specs + [row(w_diff), row(w_diff), row(w_fox), row(w_fox)]
    else:
        halo, wbd, pscale, cw = stacked[:4]
        bufs = list(stacked[4:])
        seq_block = lambda i: (i // n_tab, 0, 0)
        in_specs += [pl.BlockSpec((1,) + halo.shape[1:], seq_block), _layer_spec(wbd, layer),
                     _layer_spec(pscale, layer), _layer_spec(cw, layer)]
        args += [halo, wbd, pscale, cw]
        out_shape = [sds((m, 2 * w_fox), BF16), sds(halo.shape, F32)] + out_shape
        out_specs = [row(2 * w_fox), pl.BlockSpec((1,) + halo.shape[1:], seq_block)] + out_specs
        diff_spec = pl.BlockSpec((1, tm * n_blk, LANES), lambda i: (layer, i, 0))
        fox_spec = pl.BlockSpec((1, 1, w_fox, tm), lambda i: (layer, i // n_tab, 0, i % n_tab))
        aliases = {len(args) + k: len(out_shape) + k for k in range(len(bufs))}
        out_shape += [sds(b.shape, b.dtype) for b in bufs]
        out_specs += [diff_spec, diff_spec, fox_spec, fox_spec]
        in_specs += [pl.BlockSpec(memory_space=pl.ANY)] * len(bufs)
        args += bufs
        scratch = [pltpu.VMEM((HALO_ROWS + tm, 2 * w_fox), F32)]
    return pl.pallas_call(
        functools.partial(_in_proj_body, d=d, q_scale=q_scale, stacked=stacked is not None, n_tab=n_tab),
        out_shape=out_shape,
        grid=(m // tm,),
        in_specs=in_specs,
        out_specs=out_specs,
        scratch_shapes=scratch,
        input_output_aliases=aliases,
        compiler_params=_cparams("arbitrary"),
        name="in_proj",
    )(*args)


def _mixers_body(u_ref, halo_ref, wbd_ref, pscale_ref, cw_ref, yad_ref, state_ref, ext_ref,
                 *, tm, pos0, w):
    out_dtype = yad_ref.dtype
    ti = pl.program_id(1)

    @pl.when(ti == 0)
    def _():
        ext_ref[0:HALO_ROWS, :] = halo_ref[0]

    ya, yd, new_halo = _mix_tile(u_ref[0], ext_ref, wbd_ref[...], pscale_ref[...], cw_ref[...], pos0 + ti * tm)
    yad_ref[0, :, 0:w] = ya.astype(out_dtype)
    yad_ref[0, :, w:2 * w] = yd.astype(out_dtype)
    ext_ref[0:HALO_ROWS, :] = new_halo
    state_ref[0] = new_halo


def _mixers(u_mix, halo, wbd, pscale, cw, *, layer, tm, pos0):
    b, t, d = u_mix.shape
    w = d // 4
    return pl.pallas_call(
        functools.partial(_mixers_body, tm=tm, pos0=pos0, w=w),
        out_shape=[jax.ShapeDtypeStruct((b, t, 2 * w), BF16 if tm % 16 == 0 else F32),
                   jax.ShapeDtypeStruct((b, HALO_ROWS, 2 * w), F32)],
        grid=(b, t // tm),
        in_specs=[pl.BlockSpec((1, tm, d), lambda i, j: (i, j, 0)),
                  pl.BlockSpec((1, HALO_ROWS, 2 * w), lambda i, j: (i, 0, 0)),
                  _layer_spec(wbd, layer), _layer_spec(pscale, layer), _layer_spec(cw, layer)],
        out_specs=[pl.BlockSpec((1, tm, 2 * w), lambda i, j: (i, j, 0)),
                   pl.BlockSpec((1, HALO_ROWS, 2 * w), lambda i, j: (i, 0, 0))],
        scratch_shapes=[pltpu.VMEM((HALO_ROWS + tm, 2 * w), F32)],
        compiler_params=_cparams("parallel", "arbitrary"),
        name="mixers",
    )(u_mix, halo, wbd, pscale, cw)


def _diff_lambda(lqk_ref, lam_init):
    lq = lqk_ref[...]
    a = jnp.sum(lq[0:1, :] * lq[1:2, :], axis=1, keepdims=True)
    b = jnp.sum(lq[2:3, :] * lq[3:4, :], axis=1, keepdims=True)
    return jnp.exp(a) - jnp.exp(b) + lam_init


def _lane_cumsum(x, steps):
    lane = lax.broadcasted_iota(jnp.int32, x.shape, 1)
    s = 1
    for _ in range(steps):
        x = x + jnp.where(lane >= s, pltpu.roll(x, s, 1), 0.0)
        s *= 2
    return x


def _flash_body(*refs, tq, tk, n_split, fox, lam_init):
    if fox:
        q_ref, k_ref, v_ref, frow_ref, fcol_ref, o_ref, vt_ref, fkc_ref, s0_ref, s1_ref, acc_ref = refs
    else:
        q_ref, k_ref, v_ref, lqk_ref, gcol_ref, o_ref, vt_ref, s0_ref, s1_ref, acc_ref = refs
    qi = pl.program_id(2)
    seq = k_ref.shape[0]
    half = HEAD_DIM
    chunk = min(seq, 512)

    @pl.when(qi == 0)
    def _():
        vt_ref[LANES:, :] = jnp.ones((vt_ref.shape[0] - LANES, seq), BF16)
        for c in range(seq // chunk):
            sl = slice(c * chunk, (c + 1) * chunk)
            vt_ref[0:LANES, sl] = v_ref[sl, :].astype(F32).T.astype(BF16)
            if fox:
                fc = fcol_ref[0, 0, sl, :] * LOG2E
                fkc_ref[0, sl, :] = jnp.broadcast_to(fc[:, 0:1], (chunk, LANES))
                fkc_ref[1, sl, :] = jnp.broadcast_to(fc[:, 1:2], (chunk, LANES))

    qt = q_ref[...].astype(F32).T
    low = lax.broadcasted_iota(jnp.int32, (LANES, tq), 0) < half
    qst = jnp.concatenate([jnp.where(low, qt, 0.0), jnp.where(low, 0.0, qt)], axis=1).astype(BF16)

    acc_ref[...] = jnp.zeros(acc_ref.shape, F32)
    q0 = pl.multiple_of(qi * tq, tq)
    cw = tq // n_split
    n_chain = 2 * n_split
    lanes = [slice(c * cw, (c + 1) * cw) for c in range(n_chain)]
    if fox:
        shift = [frow_ref[0, 0, c // n_split:c // n_split + 1, pl.ds(q0 + (c % n_split) * cw, cw)] * LOG2E
                 for c in range(n_chain)]
    n_rep = cw // LANES

    def chain_mode(c, tile):
        g = c % n_split
        return "full" if (tile is None or g > tile) else ("diag" if g == tile else "skip")

    def scores(ki, s_ref, tile=None):
        k = k_ref[pl.ds(pl.multiple_of(ki * tk, tk), tk), :]
        for c in range(n_chain):
            if chain_mode(c, tile) != "skip":
                s_ref[:, lanes[c]] = _dot(k, qst[:, lanes[c]])

    def softmax_pv(ki, s_ref, ms, tile=None):
        start = pl.multiple_of(ki * tk, tk)
        vt = vt_ref[:, pl.ds(start, tk)]
        new = []
        for c in range(n_chain):
            mode = chain_mode(c, tile)
            if mode == "skip":
                new.append(ms[c])
                continue
            st = s_ref[:, lanes[c]]
            if fox:
                st = st - _rep(fkc_ref[c // n_split, pl.ds(start, tk), :], n_rep)
            if mode == "diag":
                visible = (lax.broadcasted_iota(jnp.int32, (tk, cw), 0)
                           <= lax.broadcasted_iota(jnp.int32, (tk, cw), 1))
                st = jnp.where(visible, st, NEG_INF)
            m_cur = jnp.max(st, axis=0, keepdims=True)
            if fox:
                m_cur = m_cur + shift[c]
            m_new = jnp.maximum(ms[c], m_cur)
            alpha = jnp.exp2(ms[c] - m_new)
            p = jnp.exp2(st - ((m_new - shift[c]) if fox else m_new))
            acc_ref[:, lanes[c]] = alpha * acc_ref[:, lanes[c]] + _dot(vt, p.astype(BF16))
            new.append(m_new)
        return tuple(new)

    def two_tiles(j, ms, diagonal):
        scores(2 * j + 1, s1_ref, 1 if diagonal else None)
        ms = softmax_pv(2 * j, s0_ref, ms, 0 if diagonal else None)
        if not diagonal:
            scores(2 * j + 2, s0_ref)
        return softmax_pv(2 * j + 1, s1_ref, ms, 1 if diagonal else None)

    assert cw == tk and tq == 2 * tk
    scores(0, s0_ref)
    ms = lax.fori_loop(0, qi, lambda j, c: two_tiles(j, c, False), (jnp.full((1, cw), NEG_INF, F32),) * n_chain)
    two_tiles(qi, ms, True)

    acc = acc_ref[...]
    out = acc[0:LANES, :] / acc[LANES:LANES + 1, :]
    if fox:
        res = jnp.where(low, out[:, 0:tq], out[:, tq:])
    else:
        lam = _diff_lambda(lqk_ref, lam_init)
        o = out[:, 0:tq] - lam * out[:, tq:]
        ms = jnp.mean(o * o, axis=0, keepdims=True)
        res = o * lax.rsqrt(ms + RMS_EPS) * gcol_ref[...] * (1.0 - lam_init)
    o_ref[...] = res.T.astype(BF16)


def _flash(q, k, v, extra, *, batch, seq, tq, tk, fox, lam_init=0.0):
    m, w = q.shape
    n_blk = w // LANES
    n_q = seq // tq
    q_spec = pl.BlockSpec((tq, LANES), lambda b, h, i: (b * n_q + i, h))
    kv_spec = pl.BlockSpec((seq, LANES), lambda b, h, i: (b, h))
    assert tq == 2 * tk
    ones_rows = 16
    scratch = [pltpu.VMEM((LANES + ones_rows, seq), BF16)]
    if fox:
        extra_specs = [pl.BlockSpec((1, 1, 2, seq), lambda b, h, i: (b, h, 0, 0)),
                       pl.BlockSpec((1, 1, seq, 2), lambda b, h, i: (h, b, 0, 0))]
        scratch.append(pltpu.VMEM((2, seq, LANES), F32))
    else:
        extra_specs = [_const_spec(extra[0].shape), _const_spec(extra[1].shape)]
    scratch += [pltpu.VMEM((tk, 2 * tq), F32), pltpu.VMEM((tk, 2 * tq), F32),
                pltpu.VMEM((LANES + ones_rows, 2 * tq), F32)]
    return pl.pallas_call(
        functools.partial(_flash_body, tq=tq, tk=tk, n_split=max(1, tq // 256), fox=fox, lam_init=lam_init),
        out_shape=jax.ShapeDtypeStruct((m, w), BF16),
        grid=(batch, n_blk, n_q),
        in_specs=[q_spec, kv_spec, kv_spec] + extra_specs,
        out_specs=q_spec,
        scratch_shapes=scratch,
        compiler_params=_cparams("parallel", "parallel", "arbitrary"),
        name="flash_fox" if fox else "flash_diff",
    )(q, k, v, *extra)


def _fcum_prompt_body(lf_ref, o_ref, *, steps):
    o_ref[...] = _lane_cumsum(lf_ref[...], steps)


def _fcum_prompt(logft, *, batch, seq):
    spec = pl.BlockSpec((8, seq), lambda b: (0, b))
    return pl.pallas_call(
        functools.partial(_fcum_prompt_body, steps=(seq - 1).bit_length()),
        out_shape=jax.ShapeDtypeStruct(logft.shape, F32),
        grid=(batch,),
        in_specs=[spec],
        out_specs=spec,
        compiler_params=_cparams("parallel"),
        name="fcum_prompt",
    )(logft)


def _fcum_past_body(pt_ref, *refs, pages):
    in_refs = refs[:pages]
    o_ref, x_ref, carry_ref = refs[pages:]
    depth, _, n_heads, page = in_refs[0].shape
    grp = depth * n_heads
    g = pl.program_id(1)

    @pl.when(g == 0)
    def _():
        carry_ref[...] = jnp.zeros(carry_ref.shape, F32)

    for j in range(pages):
        for d in range(depth):
            x_ref[pl.ds(j * grp + d * n_heads, n_heads), :] = in_refs[j][d, 0]
    incl = _lane_cumsum(x_ref[...], int(math.log2(page)))
    tot = jnp.broadcast_to(incl[:, page - 1:page], incl.shape)
    row = lax.broadcasted_iota(jnp.int32, incl.shape, 0)
    pre = tot
    s = grp
    while s < pages * grp:
        pre = pre + jnp.where(row >= s, pltpu.roll(pre, s, 0), 0.0)
        s *= 2
    carry = carry_ref[...]
    x_ref[...] = incl + (pre - tot) + jnp.concatenate([carry] * pages, axis=0)
    carry_ref[...] = carry + pre[(pages - 1) * grp:, :]
    for j in range(pages):
        for d in range(depth):
            o_ref[d, 0, j] = x_ref[pl.ds(j * grp + d * n_heads, n_heads), :]


def _fcum_past(logf_t, page_table, *, pages):
    depth, n_pool, n_heads, page = logf_t.shape
    n_seq, n_pages = page_table.shape
    in_specs = [pl.BlockSpec((depth, 1, n_heads, page),
                             functools.partial(lambda b, g, pt, j: (0, pt[b, g * pages + j], 0, 0), j=j))
                for j in range(pages)]
    grid_spec = pltpu.PrefetchScalarGridSpec(
        num_scalar_prefetch=1,
        grid=(n_seq, n_pages // pages),
        in_specs=in_specs,
        out_specs=pl.BlockSpec((depth, 1, pages, n_heads, page), lambda b, g, pt: (0, b, g, 0, 0)),
        scratch_shapes=[pltpu.VMEM((pages * depth * n_heads, page), F32),
                        pltpu.VMEM((depth * n_heads, page), F32)])
    return pl.pallas_call(
        functools.partial(_fcum_past_body, pages=pages),
        out_shape=jax.ShapeDtypeStruct((depth, n_seq, n_pages, n_heads, page), F32),
        grid_spec=grid_spec,
        compiler_params=_cparams("parallel", "arbitrary"),
        name="fcum_past",
    )(page_table, *([logf_t] * pages))


def _softmax_step(s, pv, m_ref, l_ref, acc_ref, shift=None):
    n_k = s.shape[1] // LANES
    n_v = acc_ref.shape[1] // LANES
    m_prev = m_ref[...]
    m_cur = jnp.max(s, axis=1, keepdims=True)
    if shift is not None:
        m_cur = m_cur + shift
    m_new = jnp.maximum(m_prev, m_cur)
    alpha = jnp.exp(m_prev - m_new)
    sub = m_new if shift is None else m_new - shift
    p = jnp.exp(s - _rep(sub, n_k))
    l_ref[...] = alpha * l_ref[...] + jnp.sum(p, axis=1, keepdims=True)
    acc_ref[...] = _rep(alpha, n_v) * acc_ref[...] + pv(p.astype(BF16))
    m_ref[...] = m_new


def _pad_rows(x, rows):
    return jnp.concatenate([x, jnp.zeros((rows - x.shape[0], x.shape[1]), x.dtype)], axis=0)


def _paged_copies(pt_ref, caches, bufs, sems, *, layer, pages, seq, step, slot):
    return [pltpu.make_async_copy(cache.at[layer, pt_ref[seq, step * pages + j]], buf.at[slot, j],
                                  sems.at[a, slot])
            for a, (cache, buf) in enumerate(zip(caches, bufs)) for j in range(pages)]


def _when(cond, fn):
    if isinstance(cond, bool):
        if cond:
            fn()
    else:
        pl.when(cond)(fn)


def _paged_pipeline(pt_ref, caches, bufs, sems, *, layer, pages, seq, step, n_seq, n_steps):
    s = seq * n_steps + step
    slot = lax.rem(s, 2)
    copies = functools.partial(_paged_copies, pt_ref, caches, bufs, sems, layer=layer, pages=pages)

    def start_own():
        for cp in copies(seq=seq, step=step, slot=slot):
            cp.start()

    def start_next():
        wrap = step + 1 == n_steps
        for cp in copies(seq=jnp.where(wrap, seq + 1, seq), step=jnp.where(wrap, 0, step + 1), slot=1 - slot):
            cp.start()

    _when(s == 0, start_own)
    _when(s + 1 < n_seq * n_steps, start_next)
    for cp in copies(seq=seq, step=step, slot=slot):
        cp.wait()
    return slot


def _decode_diff_body(pt_ref, q_ref, knew_ref, vnew_ref, lqk_ref, g_ref, k_hbm, v_hbm, o_ref,
                      kbuf_ref, vbuf_ref, sem_ref, qs_ref, m_ref, l_ref, acc_ref, *, layer, pages, **static):
    g, n_g = pl.program_id(1), pl.num_programs(1)
    slot = _paged_pipeline(pt_ref, (k_hbm, v_hbm), (kbuf_ref, vbuf_ref), sem_ref, layer=layer, pages=pages,
                           seq=pl.program_id(0), step=g, n_seq=pl.num_programs(0), n_steps=n_g)
    _decode_diff_step(q_ref, knew_ref, vnew_ref, lqk_ref, g_ref, o_ref, kbuf_ref, vbuf_ref,
                      qs_ref, m_ref, l_ref, acc_ref, slot=slot, first=g == 0, last=g == n_g - 1,
                      pages=pages, **static)


def _decode_diff_step(q_ref, knew_ref, vnew_ref, lqk_ref, g_ref, o_ref, kbuf_ref, vbuf_ref,
                      qs_ref, m_ref, l_ref, acc_ref, *, slot, first, last, pages, n_heads, t_new, lam_init,
                      between=None):
    page = kbuf_ref.shape[2] // n_heads
    hw = 2 * HEAD_DIM
    rows_h = 2 * t_new
    rows_p = 2 * rows_h

    def init():
        q = q_ref[0]
        low = lax.broadcasted_iota(jnp.int32, (t_new, hw), 1) < HEAD_DIM
        zero = jnp.zeros((t_new, hw), F32)
        pieces = []
        for h in range(n_heads):
            qh = q[:, h * hw:(h + 1) * hw]
            for piece in (jnp.where(low, qh, 0.0), jnp.where(low, 0.0, qh)):
                pieces.append(jnp.concatenate([piece, zero] if h % 2 == 0 else [zero, piece], axis=1))
        qs_ref[...] = jnp.concatenate(pieces, axis=0).astype(BF16)
        m_ref[...] = jnp.full(m_ref.shape, NEG_INF, F32)
        l_ref[...] = jnp.zeros(l_ref.shape, F32)
        acc_ref[...] = jnp.zeros(acc_ref.shape, F32)

    _when(first, init)

    def head_rows(buf_ref, h):
        return jnp.concatenate([buf_ref[slot, j, pl.ds(h, page, stride=n_heads), :].astype(BF16)
                                for j in range(pages)], axis=0)

    def pair_cols(buf_ref, hp):
        return jnp.concatenate([head_rows(buf_ref, 2 * hp), head_rows(buf_ref, 2 * hp + 1)], axis=1)

    def per_pair(fn):
        return jnp.concatenate([fn(hp, slice(hp * rows_p, (hp + 1) * rows_p)) for hp in range(n_heads // 2)],
                               axis=0)

    def own_block(x):
        first_head = lax.broadcasted_iota(jnp.int32, (rows_p, hw), 0) < rows_h
        return jnp.where(first_head, x[:, 0:hw], x[:, hw:])

    qs = qs_ref[...]
    vs = [pair_cols(vbuf_ref, hp) for hp in range(n_heads // 2)]
    s = per_pair(lambda hp, rows: _dot_nt(qs[rows, :], pair_cols(kbuf_ref, hp)))
    if between is not None:
        between()
    _softmax_step(s, lambda p: per_pair(lambda hp, rows: own_block(_dot(p[rows, :], vs[hp]))),
                  m_ref, l_ref, acc_ref)

    def finish():
        k_new = knew_ref[0]
        v_new = vnew_ref[0]
        pair = lambda x, hp: _pad_rows(x[:, hp * 2 * hw:(hp + 1) * 2 * hw], LANES).astype(BF16)
        s_new = per_pair(lambda hp, rows: _dot_nt(qs[rows, :], pair(k_new, hp)))
        row = lax.broadcasted_iota(jnp.int32, s_new.shape, 0) & (t_new - 1)
        col = lax.broadcasted_iota(jnp.int32, s_new.shape, 1)
        s_new = jnp.where(col <= row, s_new, NEG_INF)
        _softmax_step(s_new, lambda p: per_pair(lambda hp, rows: own_block(_dot(p[rows, :], pair(v_new, hp)))),
                      m_ref, l_ref, acc_ref)

        out = acc_ref[...] / l_ref[...]
        lam = _diff_lambda(lqk_ref, lam_init)
        for h in range(n_heads):
            o = out[h * rows_h:h * rows_h + t_new, :] - lam * out[h * rows_h + t_new:(h + 1) * rows_h, :]
            ms = jnp.mean(o * o, axis=-1, keepdims=True)
            o_ref[0, :, h * hw:(h + 1) * hw] = o * lax.rsqrt(ms + RMS_EPS) * g_ref[...] * (1.0 - lam_init)

    _when(last, finish)


def _decode_diff(q, k_new, v_new, cache_k, cache_v, page_table, lqk, g, *, layer, pages, lam_init):
    n_seq, t_new, width = q.shape
    n_heads = width // (2 * HEAD_DIM)
    n_pages = page_table.shape[1]
    rows = cache_k.shape[2]
    seq_spec = pl.BlockSpec((1, t_new, width), lambda b, g_, pt: (b, 0, 0))
    hbm = pl.BlockSpec(memory_space=pl.ANY)
    n_rows = n_heads * 2 * t_new
    grid_spec = pltpu.PrefetchScalarGridSpec(
        num_scalar_prefetch=1,
        grid=(n_seq, n_pages // pages),
        in_specs=[seq_spec, seq_spec, seq_spec,
                  pl.BlockSpec(lqk.shape, lambda b, g_, pt: (0, 0)),
                  pl.BlockSpec(g.shape, lambda b, g_, pt: (0, 0)), hbm, hbm],
        out_specs=seq_spec,
        scratch_shapes=[pltpu.VMEM((2, pages, rows, LANES), F32), pltpu.VMEM((2, pages, rows, LANES), F32),
                        pltpu.SemaphoreType.DMA((2, 2)),
                        pltpu.VMEM((n_rows, 2 * LANES), BF16), pltpu.VMEM((n_rows, LANES), F32),
                        pltpu.VMEM((n_rows, LANES), F32), pltpu.VMEM((n_rows, LANES), F32)])
    return pl.pallas_call(
        functools.partial(_decode_diff_body, layer=layer, pages=pages, n_heads=n_heads, t_new=t_new,
                          lam_init=lam_init),
        out_shape=jax.ShapeDtypeStruct((n_seq, t_new, width), F32),
        grid_spec=grid_spec,
        compiler_params=_cparams("arbitrary", "arbitrary"),
        name="decode_diff",
    )(page_table, q, k_new, v_new, lqk, g, cache_k, cache_v)


def _ffn_decode_diff_body(pt_ref, x_ref, wg_ref, wu_ref, wd_ref, lng_ref, lnb_ref,
                          q_ref, knew_ref, vnew_ref, lqk_ref, g_ref, k_hbm, v_hbm, o_ref, yb_ref,
                          acc_ref, kbuf_ref, vbuf_ref, sem_ref, qs_ref, m_ref, l_ref, dacc_ref,
                          *, chunk, alpha, layer, pages, n_steps, **static):
    i = pl.program_id(0)
    x = x_ref[...]
    xb = x.astype(BF16)
    n_chunks = wg_ref.shape[1] // chunk
    per = -(-n_chunks // n_steps)
    for g in range(n_steps):
        slot = _paged_pipeline(pt_ref, (k_hbm, v_hbm), (kbuf_ref, vbuf_ref), sem_ref, layer=layer, pages=pages,
                               seq=i, step=g, n_seq=pl.num_programs(0), n_steps=n_steps)
        mine = list(range(g * per, min((g + 1) * per, n_chunks)))
        k = max(1, len(mine) // 3)
        ffn = lambda cs: [_ffn_chunk(xb, wg_ref, wu_ref, wd_ref, acc_ref, c, chunk) for c in cs]
        ffn(mine[:k])
        _decode_diff_step(q_ref, knew_ref, vnew_ref, lqk_ref, g_ref, yb_ref, kbuf_ref, vbuf_ref,
                          qs_ref, m_ref, l_ref, dacc_ref, slot=slot, first=g == 0, last=g == n_steps - 1,
                          pages=pages, between=lambda: ffn(mine[k:2 * k]), **static)
        ffn(mine[2 * k:])
    z = alpha * x + 0.5 * acc_ref[...]
    o_ref[...] = _layer_norm(z, lng_ref[...], lnb_ref[...])


def _ffn_ln_decode_diff(x, wg, wu, wd, g, b, q, k_new, v_new, cache_k, cache_v, page_table, lqk, sg,
                        *, layer, tm, alpha, pages, lam_init):
    m, d = x.shape
    n_seq, t_new, width = q.shape
    assert m // tm == n_seq
    n_heads = width // (2 * HEAD_DIM)
    n_steps = page_table.shape[1] // pages
    rows = cache_k.shape[2]
    row = pl.BlockSpec((tm, d), lambda i, pt: (i, 0))
    seq_spec = pl.BlockSpec((1, t_new, width), lambda i, pt: (i, 0, 0))
    hbm = pl.BlockSpec(memory_space=pl.ANY)
    n_rows = n_heads * 2 * t_new
    grid_spec = pltpu.PrefetchScalarGridSpec(
        num_scalar_prefetch=1,
        grid=(n_seq,),
        in_specs=[row, _layer_spec(wg, layer), _layer_spec(wu, layer), _layer_spec(wd, layer),
                  _const_spec((1, d)), _const_spec((1, d)), seq_spec, seq_spec, seq_spec,
                  _const_spec(lqk.shape), _const_spec(sg.shape), hbm, hbm],
        out_specs=[row, seq_spec],
        scratch_shapes=[pltpu.VMEM((tm, d), F32),
                        pltpu.VMEM((2, pages, rows, LANES), F32), pltpu.VMEM((2, pages, rows, LANES), F32),
                        pltpu.SemaphoreType.DMA((2, 2)),
                        pltpu.VMEM((n_rows, 2 * LANES), BF16), pltpu.VMEM((n_rows, LANES), F32),
                        pltpu.VMEM((n_rows, LANES), F32), pltpu.VMEM((n_rows, LANES), F32)])
    return pl.pallas_call(
        functools.partial(_ffn_decode_diff_body, chunk=FFN_CHUNK, alpha=alpha, layer=layer, pages=pages,
                          n_steps=n_steps, n_heads=n_heads, t_new=t_new, lam_init=lam_init),
        out_shape=[jax.ShapeDtypeStruct((m, d), F32), jax.ShapeDtypeStruct((n_seq, t_new, width), F32)],
        grid_spec=grid_spec,
        compiler_params=_cparams("arbitrary"),
        name="ffn_ln_decode_diff",
    )(page_table, x, wg, wu, wd, g, b, q, k_new, v_new, lqk, sg, cache_k, cache_v)


def _decode_fox_body(pt_ref, q_ref, knew_ref, vnew_ref, lfnew_ref, fk_ref, kt_hbm, vt_hbm, o_ref,
                     ktbuf_ref, vtbuf_ref, sem_ref, qs_ref, m_ref, l_ref, acc_ref, *, layer, pages, **static):
    g, n_g = pl.program_id(1), pl.num_programs(1)
    slot = _paged_pipeline(pt_ref, (kt_hbm, vt_hbm), (ktbuf_ref, vtbuf_ref), sem_ref, layer=layer, pages=pages,
                           seq=pl.program_id(0), step=g, n_seq=pl.num_programs(0), n_steps=n_g)
    _decode_fox_step(q_ref, knew_ref, vnew_ref, lfnew_ref, fk_ref, o_ref, ktbuf_ref, vtbuf_ref,
                     qs_ref, m_ref, l_ref, acc_ref, slot=slot, first=g == 0, last=g == n_g - 1,
                     page_base=g * pages, pages=pages, **static)


def _decode_fox_step(q_ref, knew_ref, vnew_ref, lfnew_ref, fk_ref, o_ref, ktbuf_ref, vtbuf_ref,
                     qs_ref, m_ref, l_ref, acc_ref, *, slot, first, last, page_base, pages, n_heads, t_new,
                     between=None):
    width = q_ref.shape[2]
    page = ktbuf_ref.shape[3]
    n_rows = n_heads * t_new

    def init():
        q = q_ref[0]
        lane = lax.broadcasted_iota(jnp.int32, (t_new, width), 1)
        qs_ref[...] = jnp.concatenate(
            [jnp.where((lane >= h * HEAD_DIM) & (lane < (h + 1) * HEAD_DIM), q, 0.0) for h in range(n_heads)],
            axis=0).astype(BF16)
        m_ref[...] = jnp.full(m_ref.shape, NEG_INF, F32)
        l_ref[...] = jnp.zeros(l_ref.shape, F32)
        acc_ref[...] = jnp.zeros(acc_ref.shape, F32)

    _when(first, init)

    def head_rows(per_head):
        return jnp.concatenate([jnp.broadcast_to(per_head(h), (t_new, per_head(h).shape[1]))
                                for h in range(n_heads)], axis=0)

    total = fk_ref[0, 0, fk_ref.shape[2] - 1, :, page - 1:page]
    f_new = _lane_cumsum(lfnew_ref[0], int(math.log2(t_new))) + jnp.concatenate(
        [total, jnp.zeros((8 - n_heads, 1), F32)], axis=0)
    eye = (lax.broadcasted_iota(jnp.int32, (t_new, LANES), 0)
           == lax.broadcasted_iota(jnp.int32, (t_new, LANES), 1))
    fq_col = jnp.concatenate(
        [jnp.sum(jnp.where(eye, jnp.broadcast_to(f_new[h:h + 1, :], (t_new, LANES)), 0.0),
                 axis=1, keepdims=True) for h in range(n_heads)], axis=0)
    shift = jnp.broadcast_to(fq_col, (n_rows, LANES))

    n_chain = m_ref.shape[0]
    per = pages // n_chain
    qs = qs_ref[...]
    scores = []
    for c in range(n_chain):
        sel = range(c * per, (c + 1) * per)
        kt = jnp.concatenate([ktbuf_ref[slot, j].astype(BF16) for j in sel], axis=1)
        fk = jnp.concatenate([head_rows(lambda h: fk_ref[0, 0, page_base + j, h:h + 1, :]) for j in sel], axis=1)
        scores.append(_dot(qs, kt) - fk)
    if between is not None:
        between()
    for c in range(n_chain):
        vt = jnp.concatenate([vtbuf_ref[slot, j].astype(BF16) for j in range(c * per, (c + 1) * per)], axis=1)
        _softmax_step(scores[c], lambda p, vt=vt: _dot_nt(p, vt), m_ref.at[c], l_ref.at[c], acc_ref.at[c], shift)

    def finish():
        k_new = _pad_rows(knew_ref[0], LANES).astype(BF16)
        v_new = _pad_rows(vnew_ref[0], LANES).astype(BF16)
        s_new = _dot_nt(qs, k_new) - head_rows(lambda h: f_new[h:h + 1, :])
        row = lax.broadcasted_iota(jnp.int32, s_new.shape, 0) & (t_new - 1)
        col = lax.broadcasted_iota(jnp.int32, s_new.shape, 1)
        s_new = jnp.where(col <= row, s_new, NEG_INF)
        _softmax_step(s_new, lambda p: _dot(p, v_new), m_ref.at[0], l_ref.at[0], acc_ref.at[0], shift)

        m_all = m_ref[0]
        for c in range(1, n_chain):
            m_all = jnp.maximum(m_all, m_ref[c])
        l_all = jnp.zeros(m_all.shape, F32)
        acc_all = jnp.zeros(acc_ref.shape[1:], F32)
        for c in range(n_chain):
            wgt = jnp.exp(m_ref[c] - m_all)
            l_all = l_all + wgt * l_ref[c]
            acc_all = acc_all + _rep(wgt, width // LANES) * acc_ref[c]
        out = acc_all / _rep(l_all, width // LANES)
        lane = lax.broadcasted_iota(jnp.int32, (t_new, width), 1)
        res = jnp.zeros((t_new, width), F32)
        for h in range(n_heads):
            keep = (lane >= h * HEAD_DIM) & (lane < (h + 1) * HEAD_DIM)
            res = jnp.where(keep, out[h * t_new:(h + 1) * t_new, :], res)
        o_ref[0] = res

    _when(last, finish)


def _decode_fox_pieces(q, cache_kt, page_table, fk, *, layer, pages, seq_index):
    n_seq, t_new, width = q.shape
    n_heads = width // HEAD_DIM
    page = cache_kt.shape[3]
    n_pages = page_table.shape[1]
    seq_spec = pl.BlockSpec((1, t_new, width), lambda *a: (seq_index(*a), 0, 0))
    hbm = pl.BlockSpec(memory_space=pl.ANY)
    n_rows = n_heads * t_new
    n_chain = 2 if pages % 2 == 0 else 1
    in_specs = [seq_spec, seq_spec, seq_spec,
                pl.BlockSpec((1, 8, LANES), lambda *a: (seq_index(*a), 0, 0)),
                pl.BlockSpec((1, 1, n_pages, n_heads, page), lambda *a: (layer, seq_index(*a), 0, 0, 0)),
                hbm, hbm]
    scratch = [pltpu.VMEM((2, pages, width, page), F32), pltpu.VMEM((2, pages, width, page), F32),
               pltpu.SemaphoreType.DMA((2, 2)),
               pltpu.VMEM((n_rows, width), BF16), pltpu.VMEM((n_chain, n_rows, LANES), F32),
               pltpu.VMEM((n_chain, n_rows, LANES), F32), pltpu.VMEM((n_chain, n_rows, width), F32)]
    return in_specs, seq_spec, scratch, dict(n_heads=n_heads, t_new=t_new)


def _decode_fox(q, k_new, v_new, cache_kt, cache_vt, page_table, lf_new, fk, *, layer, pages):
    n_seq, t_new, width = q.shape
    in_specs, seq_spec, scratch, static = _decode_fox_pieces(
        q, cache_kt, page_table, fk, layer=layer, pages=pages, seq_index=lambda b, g, pt: b)
    grid_spec = pltpu.PrefetchScalarGridSpec(
        num_scalar_prefetch=1, grid=(n_seq, page_table.shape[1] // pages),
        in_specs=in_specs, out_specs=seq_spec, scratch_shapes=scratch)
    return pl.pallas_call(
        functools.partial(_decode_fox_body, layer=layer, pages=pages, **static),
        out_shape=jax.ShapeDtypeStruct((n_seq, t_new, width), F32),
        grid_spec=grid_spec,
        compiler_params=_cparams("arbitrary", "arbitrary"),
        name="decode_fox",
    )(page_table, q, k_new, v_new, lf_new, fk, cache_kt, cache_vt)


def _ffn_decode_fox_body(pt_ref, x_ref, wg_ref, wu_ref, wd_ref, lng_ref, lnb_ref,
                         q_ref, knew_ref, vnew_ref, lfnew_ref, fk_ref, kt_hbm, vt_hbm, o_ref, yc_ref,
                         acc_ref, ktbuf_ref, vtbuf_ref, sem_ref, qs_ref, m_ref, l_ref, dacc_ref,
                         *, chunk, alpha, layer, pages, n_steps, **static):
    i = pl.program_id(0)
    x = x_ref[...]
    xb = x.astype(BF16)
    n_chunks = wg_ref.shape[1] // chunk
    per = -(-n_chunks // n_steps)
    for g in range(n_steps):
        slot = _paged_pipeline(pt_ref, (kt_hbm, vt_hbm), (ktbuf_ref, vtbuf_ref), sem_ref, layer=layer,
                               pages=pages, seq=i, step=g, n_seq=pl.num_programs(0), n_steps=n_steps)
        mine = list(range(g * per, min((g + 1) * per, n_chunks)))
        k = max(1, len(mine) // 3)
        ffn = lambda cs: [_ffn_chunk(xb, wg_ref, wu_ref, wd_ref, acc_ref, c, chunk) for c in cs]
        ffn(mine[:k])
        _decode_fox_step(q_ref, knew_ref, vnew_ref, lfnew_ref, fk_ref, yc_ref, ktbuf_ref, vtbuf_ref,
                         qs_ref, m_ref, l_ref, dacc_ref, slot=slot, first=g == 0, last=g == n_steps - 1,
                         page_base=g * pages, pages=pages, between=lambda: ffn(mine[k:2 * k]), **static)
        ffn(mine[2 * k:])
    z = alpha * x + 0.5 * acc_ref[...]
    o_ref[...] = _layer_norm(z, lng_ref[...], lnb_ref[...])


def _ffn_ln_decode_fox(x, wg, wu, wd, g, b, q, k_new, v_new, cache_kt, cache_vt, page_table, lf_new, fk,
                       *, layer, tm, alpha, pages):
    m, d = x.shape
    n_seq, t_new, width = q.shape
    assert m // tm == n_seq
    dec_specs, seq_spec, dec_scratch, static = _decode_fox_pieces(
        q, cache_kt, page_table, fk, layer=layer, pages=pages, seq_index=lambda i, pt: i)
    row = pl.BlockSpec((tm, d), lambda i, pt: (i, 0))
    grid_spec = pltpu.PrefetchScalarGridSpec(
        num_scalar_prefetch=1,
        grid=(n_seq,),
        in_specs=[row, _layer_spec(wg, layer), _layer_spec(wu, layer), _layer_spec(wd, layer),
                  _const_spec((1, d)), _const_spec((1, d))] + dec_specs,
        out_specs=[row, seq_spec],
        scratch_shapes=[pltpu.VMEM((tm, d), F32)] + dec_scratch)
    return pl.pallas_call(
        functools.partial(_ffn_decode_fox_body, chunk=FFN_CHUNK, alpha=alpha, layer=layer, pages=pages,
                          n_steps=page_table.shape[1] // pages, **static),
        out_shape=[jax.ShapeDtypeStruct((m, d), F32), jax.ShapeDtypeStruct((n_seq, t_new, width), F32)],
        grid_spec=grid_spec,
        compiler_params=_cparams("arbitrary"),
        name="ffn_ln_decode_fox",
    )(page_table, x, wg, wu, wd, g, b, q, k_new, v_new, lf_new, fk, cache_kt, cache_vt)


def _merge_body(x_ref, yad_ref, yb_ref, yc_ref, wg_ref, wp_ref, wo_ref, g_ref, b_ref, o_ref, *, alpha):
    x = x_ref[...]
    xb = x.astype(BF16)
    d = x.shape[1]
    w = d // 4
    yad = yad_ref[...]
    ys = (yad[:, 0:w], yb_ref[...], yc_ref[...], yad[:, w:2 * w])
    merged = None
    row = 0
    for i, y in enumerate(ys):
        gate = jax.nn.sigmoid(_dot(xb, wg_ref[:, i * d:(i + 1) * d]))
        term = gate * _dot(y.astype(BF16), wp_ref[row:row + y.shape[1], :])
        merged = term if merged is None else merged + term
        row += y.shape[1]
    out = _dot(merged.astype(BF16), wo_ref[...])
    o_ref[...] = _layer_norm(alpha * x + out, g_ref[...], b_ref[...])


def _merge(x, yad, yb, yc, wg, wp, wo, g, b, *, layer, tm, alpha):
    m, d = x.shape
    row = lambda w: pl.BlockSpec((tm, w), lambda i: (i, 0))
    return pl.pallas_call(
        functools.partial(_merge_body, alpha=alpha),
        out_shape=jax.ShapeDtypeStruct((m, d), F32),
        grid=(m // tm,),
        in_specs=[row(d), row(yad.shape[1]), row(yb.shape[1]), row(yc.shape[1]),
                  _layer_spec(wg, layer), _layer_spec(wp, layer), _layer_spec(wo, layer),
                  _const_spec((1, d)), _const_spec((1, d))],
        out_specs=row(d),
        compiler_params=_cparams("parallel"),
        name="merge",
    )(x, yad, yb, yc, wg, wp, wo, g, b)


def _rope_tables(pos):
    half = HEAD_DIM // 2
    inv = ROPE_THETA ** (-jnp.arange(half, dtype=F32) / half)
    ang = pos.astype(F32)[:, None] * inv[None, :]
    cos, sin = jnp.cos(ang), jnp.sin(ang)
    cos_t = jnp.tile(cos, (1, LANES // half))
    sin_t = jnp.tile(jnp.concatenate([-sin, sin], axis=1), (1, LANES // HEAD_DIM))
    return cos_t, sin_t


def _pick_tile(n, target):
    t = min(n, target)
    while n % t:
        t //= 2
    return t


def kernel(x_prompt, x_sample, cache_diff_k, cache_diff_v, cache_fox_k, cache_fox_v, cache_fox_logf,
           state_pool, state_conv, page_table, w_in, b_fgate, pool_w, pool_scale, lambda_qk, subln_g,
           conv_w, w_branch, w_o, w_ffn1_gate, w_ffn1_up, w_ffn1_down, w_ffn2_gate, w_ffn2_up,
           w_ffn2_down, ln_g, ln_b):
    batch, seq, d = x_prompt.shape
    n_seq, t_new, _ = x_sample.shape
    depth = w_in.shape[0]
    n_pool, page = cache_diff_k.shape[1:3]
    past_len = page_table.shape[1] * page
    h_diff = cache_diff_k.shape[3]
    h_fox = cache_fox_k.shape[3]
    w_pool = state_pool.shape[-1]
    w_diff, w_fox, w_conv = d // 2, d // 4, d // 4
    alpha = (2.0 * depth) ** 0.25

    sizes = (w_pool, w_diff, w_diff, w_diff, w_fox, w_fox, w_fox, h_fox, w_conv, w_conv, w_conv, 4 * d)
    offs = np.concatenate([[0], np.cumsum(sizes)])
    col = lambda i: w_in[:, :, offs[i]:offs[i + 1]]
    w_main = jnp.concatenate([col(0), col(8), col(9), col(10), col(1), col(2), col(3), col(4), col(5), col(6)],
                             axis=-1).astype(BF16)
    w_ft = jnp.pad(jnp.swapaxes(col(7), 1, 2), ((0, 0), (0, 16 - h_fox), (0, 0))).astype(BF16)
    b_f = jnp.pad(b_fgate, ((0, 0), (0, 8 - h_fox)))[:, :, None]
    w_gate = col(11).astype(BF16)
    w_br = w_branch.astype(BF16)
    w_out = w_o.astype(BF16)
    ffn = [tuple(w.astype(BF16) for w in ws) for ws in
           ((w_ffn1_gate, w_ffn1_up, w_ffn1_down), (w_ffn2_gate, w_ffn2_up, w_ffn2_down))]
    n_grp, grp = pool_w.shape[1], pool_w.shape[2]
    wbd = jnp.zeros((depth, w_pool, w_pool), F32)
    for gidx in range(n_grp):
        wbd = wbd.at[:, gidx * grp:(gidx + 1) * grp, gidx * grp:(gidx + 1) * grp].set(pool_w[:, gidx])
    wbd = wbd.astype(BF16)
    pscale = pool_scale[:, None, :]

    cos_p, sin_p = _rope_tables(jnp.arange(seq, dtype=jnp.int32))
    cos_s, sin_s = _rope_tables(past_len + jnp.arange(t_new, dtype=jnp.int32))
    cos_s, sin_s = jnp.tile(cos_s, (n_seq, 1)), jnp.tile(sin_s, (n_seq, 1))

    ck_diff = cache_diff_k.reshape(depth, n_pool, page * h_diff, 2 * HEAD_DIM)
    cv_diff = cache_diff_v.reshape(depth, n_pool, page * h_diff, 2 * HEAD_DIM)
    ck_fox = jnp.transpose(cache_fox_k, (0, 1, 3, 4, 2)).reshape(depth, n_pool, h_fox * HEAD_DIM, page)
    cv_fox = jnp.transpose(cache_fox_v, (0, 1, 3, 4, 2)).reshape(depth, n_pool, h_fox * HEAD_DIM, page)
    pages = _pick_tile(page_table.shape[1], 16)
    fk_past = _fcum_past(jnp.transpose(cache_fox_logf, (0, 1, 3, 2)), page_table, pages=pages)

    halo_p = jnp.zeros((batch, HALO_ROWS, 2 * w_pool), F32)

    tm_p = _pick_tile(batch * seq, 512)
    tm_s = n_seq * t_new
    tq = _pick_tile(seq, 512)
    tk = tq // 2

    pages_fox = _pick_tile(page_table.shape[1], 32)
    p_kv = [jnp.zeros((depth, batch * seq * h_diff, 2 * HEAD_DIM), F32) for _ in range(2)]
    p_kv += [jnp.zeros((depth, batch, h_fox * HEAD_DIM, seq), F32) for _ in range(2)]

    xp = x_prompt.reshape(batch * seq, d)
    xs = x_sample.reshape(n_seq * t_new, d)
    p_rows, s_rows = [], []
    for l in range(depth):
        lam_init = 0.8 - 0.6 * math.exp(-0.3 * l)
        lng = lambda i: ln_g[l, i][None, :]
        lnb = lambda i: ln_b[l, i][None, :]
        halo_s = jnp.concatenate(
            [jnp.pad(state_pool[l], ((0, 0), (HALO_ROWS - state_pool.shape[2], 0), (0, 0))),
             jnp.pad(state_conv[l], ((0, 0), (HALO_ROWS - state_conv.shape[2], 0), (0, 0)))], axis=-1)

        xs = _ffn_ln(xs, *ffn[0], lng(0), lnb(0), layer=l, tm=tm_s, alpha=alpha)
        u_mix, s_qd, _, _, s_qf, _, _, s_logft, s_kd, s_vd, s_kf, s_vf = _in_proj(
            xs, w_main, w_ft, b_f, cos_s, sin_s, layer=l, tm=tm_s, q_scale=1.0 / math.sqrt(HEAD_DIM))
        s_yad, s_state = _mixers(u_mix.reshape(n_seq, t_new, d), halo_s, wbd, pscale, conv_w,
                                 layer=l, tm=t_new, pos0=past_len)
        seq3 = lambda a: a.reshape(n_seq, t_new, a.shape[-1])
        diff_args = (seq3(s_qd.astype(F32)), seq3(s_kd), seq3(s_vd), ck_diff, cv_diff, page_table,
                     lambda_qk[l], subln_g[l][None, :])

        if (batch * seq) // tm_p == n_seq:
            xp, s_yb = _ffn_ln_decode_diff(xp, *ffn[0], lng(0), lnb(0), *diff_args, layer=l, tm=tm_p,
                                           alpha=alpha, pages=pages, lam_init=lam_init)
        else:
            xp = _ffn_ln(xp, *ffn[0], lng(0), lnb(0), layer=l, tm=tm_p, alpha=alpha)
            s_yb = _decode_diff(*diff_args, layer=l, pages=pages, lam_init=lam_init)
        yad, state, qd, kd16, vd16, qf, kf16, vf16, logft, *p_kv = _in_proj(
            xp, w_main, w_ft, b_f, cos_p, sin_p, layer=l, tm=tm_p,
            q_scale=LOG2E / math.sqrt(HEAD_DIM),
            stacked=(halo_p, wbd, pscale, conv_w, *p_kv))
        yb = _flash(qd, kd16, vd16, (lambda_qk[l], subln_g[l][:, None]), batch=batch, seq=seq, tq=tq, tk=tk,
                    fox=False, lam_init=lam_init)
        logf = logft[:h_fox].reshape(h_fox, batch, seq)
        f_cum = _fcum_prompt(logft, batch=batch, seq=seq)[:h_fox].reshape(h_fox // 2, 2, batch, seq)
        f_row = jnp.transpose(f_cum, (2, 0, 1, 3))
        f_col = jnp.transpose(f_cum, (0, 2, 3, 1))
        yc = _flash(qf, kf16, vf16, (f_row, f_col), batch=batch, seq=seq, tq=tq, tk=tk, fox=True)
        xp = _merge(xp, yad, yb, yc, w_gate, w_br, w_out, lng(1), lnb(1), layer=l, tm=tm_p, alpha=alpha)
        s_logf = s_logft[:h_fox].reshape(h_fox, n_seq, t_new)
        lf_new = jnp.pad(jnp.transpose(s_logf, (1, 0, 2)), ((0, 0), (0, 8 - h_fox), (0, LANES - t_new)))
        fox_args = (seq3(s_qf.astype(F32)), seq3(s_kf), seq3(s_vf), ck_fox, cv_fox, page_table, lf_new, fk_past)
        if (batch * seq) // tm_p == n_seq:
            xp, s_yc = _ffn_ln_decode_fox(xp, *ffn[1], lng(2), lnb(2), *fox_args, layer=l, tm=tm_p,
                                          alpha=alpha, pages=pages_fox)
        else:
            xp = _ffn_ln(xp, *ffn[1], lng(2), lnb(2), layer=l, tm=tm_p, alpha=alpha)
            s_yc = _decode_fox(*fox_args, layer=l, pages=pages_fox)
        p_rows.append((jnp.transpose(logf, (1, 2, 0)),
                       state[:, 1:, :w_pool], state[:, HALO_ROWS - state_conv.shape[2]:, w_pool:]))

        xs = _merge(xs, s_yad.reshape(n_seq * t_new, 2 * w_pool), s_yb.reshape(n_seq * t_new, -1),
                    s_yc.reshape(n_seq * t_new, -1), w_gate, w_br, w_out, lng(1), lnb(1), layer=l, tm=tm_s,
                    alpha=alpha)
        xs = _ffn_ln(xs, *ffn[1], lng(2), lnb(2), layer=l, tm=tm_s, alpha=alpha)
        s_rows.append((s_kd.reshape(n_seq, t_new, h_diff, -1), s_vd.reshape(n_seq, t_new, h_diff, -1),
                       s_kf.reshape(n_seq, t_new, h_fox, -1), s_vf.reshape(n_seq, t_new, h_fox, -1),
                       jnp.transpose(s_logf, (1, 2, 0)),
                       s_state[:, 1:, :w_pool], s_state[:, HALO_ROWS - state_conv.shape[2]:, w_pool:]))

    p_small = [jnp.stack([r[i] for r in p_rows], axis=0) for i in range(3)]
    s_out = [jnp.stack([r[i] for r in s_rows], axis=0) for i in range(7)]
    kd_all, vd_all, kf_all, vf_all = p_kv
    fox_rows = lambda a: jnp.transpose(a.reshape(depth, batch, h_fox, HEAD_DIM, seq), (0, 1, 4, 2, 3))
    return (xp.reshape(batch, seq, d), xs.reshape(n_seq, t_new, d),
            kd_all.reshape(depth, batch, seq, h_diff, 2 * HEAD_DIM),
            vd_all.reshape(depth, batch, seq, h_diff, 2 * HEAD_DIM),
            fox_rows(kf_all), fox_rows(vf_all), *p_small, *s_out)
```

```python
import functools
import math

import jax
import jax.numpy as jnp
import numpy as np
from jax import lax
from jax.experimental import pallas as pl
from jax.experimental.pallas import tpu as pltpu

F32 = jnp.float32
BF16 = jnp.bfloat16

HEAD_DIM = 64
ROPE_THETA = 10000.0
LN_EPS = 1e-5
RMS_EPS = 1e-5
NEG_INF = -1e30
LOG2E = math.log2(math.e)
POOL_WINDOWS = (2, 4, 8, 16)
HALO_ROWS = 16
LANES = 128
FFN_CHUNK = 256
VMEM_LIMIT_BYTES = 56 * 1024 * 1024


def _cparams(*sem):
    return pltpu.CompilerParams(dimension_semantics=sem, vmem_limit_bytes=VMEM_LIMIT_BYTES)


def _const_spec(shape):
    zeros = (0,) * len(shape)
    return pl.BlockSpec(shape, lambda *_: zeros, pipeline_mode=pl.Buffered(1))


def _layer_spec(arr, layer):
    zeros = (0,) * (arr.ndim - 1)
    return pl.BlockSpec((None,) + arr.shape[1:], lambda *_: (layer,) + zeros, pipeline_mode=pl.Buffered(1))


def _dot(a, b):
    return jnp.dot(a, b, preferred_element_type=F32)


def _dot_nt(a, b):
    return lax.dot_general(a, b, (((1,), (1,)), ((), ())), preferred_element_type=F32)


def _rep(x, n):
    return x if n == 1 else jnp.concatenate([x] * n, axis=1)


def _layer_norm(z, g, b):
    mu = jnp.mean(z, axis=-1, keepdims=True)
    zc = z - mu
    var = jnp.mean(zc * zc, axis=-1, keepdims=True)
    return zc * lax.rsqrt(var + LN_EPS) * g + b


def _ffn_chunk(xb, wg_ref, wu_ref, wd_ref, acc_ref, c, chunk, assign_first=True):
    sl = slice(c * chunk, (c + 1) * chunk)
    gate = _dot(xb, wg_ref[:, sl])
    up = _dot(xb, wu_ref[:, sl])
    h = gate * jax.nn.sigmoid(gate) * up
    d = _dot(h.astype(BF16), wd_ref[sl, :])
    if c == 0 and assign_first:
        acc_ref[...] = d
    else:
        acc_ref[...] += d


def _ffn_ln_body(x_ref, wg_ref, wu_ref, wd_ref, g_ref, b_ref, o_ref, acc_ref, *, chunk, alpha):
    x = x_ref[...]
    xb = x.astype(BF16)
    for c in range(wg_ref.shape[1] // chunk):
        _ffn_chunk(xb, wg_ref, wu_ref, wd_ref, acc_ref, c, chunk)
    z = alpha * x + 0.5 * acc_ref[...]
    o_ref[...] = _layer_norm(z, g_ref[...], b_ref[...])


def _ffn_ln_stream_body(x_ref, wg_ref, wu_ref, wd_ref, g_ref, b_ref, o_ref, acc_ref, *, chunk, alpha):
    f = pl.program_id(1)

    @pl.when(f == 0)
    def _():
        acc_ref[...] = jnp.zeros(acc_ref.shape, F32)

    x = x_ref[...]
    xb = x.astype(BF16)
    for c in range(wg_ref.shape[1] // chunk):
        _ffn_chunk(xb, wg_ref, wu_ref, wd_ref, acc_ref, c, chunk, assign_first=False)

    @pl.when(f == pl.num_programs(1) - 1)
    def _():
        z = alpha * x + 0.5 * acc_ref[...]
        o_ref[...] = _layer_norm(z, g_ref[...], b_ref[...])


def _ffn_ln_stream(x, wg, wu, wd, g, b, *, layer, tm, tf, alpha):
    m, d = x.shape
    d_ff = wg.shape[2]
    row = pl.BlockSpec((tm, d), lambda i, f: (i, 0))
    return pl.pallas_call(
        functools.partial(_ffn_ln_stream_body, chunk=tf, alpha=alpha),
        out_shape=jax.ShapeDtypeStruct((m, d), F32),
        grid=(m // tm, d_ff // tf),
        in_specs=[row, pl.BlockSpec((None, d, tf), lambda i, f: (layer, 0, f)),
                  pl.BlockSpec((None, d, tf), lambda i, f: (layer, 0, f)),
                  pl.BlockSpec((None, tf, d), lambda i, f: (layer, f, 0)),
                  _const_spec((1, d)), _const_spec((1, d))],
        out_specs=row,
        scratch_shapes=[pltpu.VMEM((tm, d), F32)],
        compiler_params=_cparams("parallel", "arbitrary"),
        name="ffn_ln_stream",
    )(x, wg, wu, wd, g, b)


def _ffn_ln(x, wg, wu, wd, g, b, *, layer, tm, alpha):
    m, d = x.shape
    row = pl.BlockSpec((tm, d), lambda i: (i, 0))
    return pl.pallas_call(
        functools.partial(_ffn_ln_body, chunk=FFN_CHUNK, alpha=alpha),
        out_shape=jax.ShapeDtypeStruct((m, d), F32),
        grid=(m // tm,),
        in_specs=[row, _layer_spec(wg, layer), _layer_spec(wu, layer), _layer_spec(wd, layer),
                  _const_spec((1, d)), _const_spec((1, d))],
        out_specs=row,
        scratch_shapes=[pltpu.VMEM((tm, d), F32)],
        compiler_params=_cparams("parallel"),
        name="ffn_ln",
    )(x, wg, wu, wd, g, b)


def _mix_tile(u, ext_ref, wbd, pscale, cw, pos_start):
    tm = u.shape[0]
    w = u.shape[1] // 4
    up = u[:, 0:w]
    cb = u[:, w:2 * w]
    ext_ref[HALO_ROWS:HALO_ROWS + tm, 0:w] = up
    ext_ref[HALO_ROWS:HALO_ROWS + tm, w:2 * w] = u[:, 2 * w:3 * w] * u[:, 3 * w:4 * w]
    e = ext_ref[...]
    ep = e[:, 0:w]
    ev = e[:, w:2 * w]

    sums = []
    b = ep
    for k in (1, 2, 4, 8):
        b = b + pltpu.roll(b, k, 0)
        sums.append(b[HALO_ROWS:, :])
    lane = lax.broadcasted_iota(jnp.int32, (tm, w), 1)
    grp = w // len(POOL_WINDOWS)
    win_sum = jnp.where(lane < grp, sums[0],
                        jnp.where(lane < 2 * grp, sums[1], jnp.where(lane < 3 * grp, sums[2], sums[3])))
    window = jnp.where(lane < grp, POOL_WINDOWS[0],
                       jnp.where(lane < 2 * grp, POOL_WINDOWS[1],
                                 jnp.where(lane < 3 * grp, POOL_WINDOWS[2], POOL_WINDOWS[3])))
    pos = pos_start + lax.broadcasted_iota(jnp.int32, (tm, w), 0)
    cnt = jnp.minimum(pos + 1, window).astype(F32)
    pooled = win_sum / cnt - up
    ya = _dot(pooled.astype(BF16), wbd) * pscale

    conv = cw[0:1, :] * pltpu.roll(ev, 2, 0) + cw[1:2, :] * pltpu.roll(ev, 1, 0) + cw[2:3, :] * ev
    yd = cb * conv[HALO_ROWS:, :]
    return ya, yd, e[tm:tm + HALO_ROWS, :]


def _in_proj_body(*refs, d, q_scale, stacked, n_tab):
    if stacked:
        (x_ref, w_ref, wft_ref, bf_ref, cos_ref, sin_ref, halo_ref, wbd_ref, pscale_ref, cw_ref,
         _, _, _, _,
         yad_ref, state_ref, qd_ref, kd16_ref, vd16_ref, qf_ref, kf16_ref, vf16_ref, logft_ref,
         kd32_ref, vd32_ref, kf32_ref, vf32_ref, ext_ref) = refs
    else:
        (x_ref, w_ref, wft_ref, bf_ref, cos_ref, sin_ref,
         umix_ref, qd_ref, kd16_ref, vd16_ref, qf_ref, kf16_ref, vf16_ref, logft_ref,
         kd32_ref, vd32_ref, kf32_ref, vf32_ref) = refs
    w_mix, w_diff, w_fox = d, d // 2, d // 4
    n_blk = w_diff // LANES
    xb = x_ref[...].astype(BF16)
    tm = xb.shape[0]

    def put_diff(ref, j, val):
        if stacked:
            ref[0, pl.ds(j, tm, stride=n_blk), :] = val
        else:
            ref[:, j * LANES:(j + 1) * LANES] = val

    def put_fox(ref, val):
        if stacked:
            ref[0, 0] = val.T
        else:
            ref[...] = val

    u_mix = _dot(xb, w_ref[:, 0:w_mix])
    if stacked:
        ti = pl.program_id(0) % n_tab

        @pl.when(ti == 0)
        def _():
            ext_ref[0:HALO_ROWS, :] = halo_ref[0]

        ya, yd, new_halo = _mix_tile(u_mix, ext_ref, wbd_ref[...], pscale_ref[...], cw_ref[...], ti * tm)
        yad_ref[:, 0:w_fox] = ya.astype(BF16)
        yad_ref[:, w_fox:2 * w_fox] = yd.astype(BF16)
        ext_ref[0:HALO_ROWS, :] = new_halo
        state_ref[0] = new_halo
    else:
        umix_ref[...] = u_mix

    qk = _dot(xb, w_ref[:, w_mix:w_mix + 2 * w_diff])
    cos = cos_ref[...]
    sin = sin_ref[...]
    lane = lax.broadcasted_iota(jnp.int32, (tm, LANES), 1)
    first_half = (lane & (HEAD_DIM // 2)) == 0
    for j in range(2 * n_blk):
        blk = qk[:, j * LANES:(j + 1) * LANES]
        partner = jnp.where(first_half, pltpu.roll(blk, LANES - HEAD_DIM // 2, 1),
                            pltpu.roll(blk, HEAD_DIM // 2, 1))
        r = blk * cos + partner * sin
        if j < n_blk:
            qd_ref[:, j * LANES:(j + 1) * LANES] = (r * q_scale).astype(BF16)
        else:
            put_diff(kd32_ref, j - n_blk, r)
            kd16_ref[:, (j - n_blk) * LANES:(j - n_blk + 1) * LANES] = r.astype(BF16)

    off = w_mix + 2 * w_diff
    vd = _dot(xb, w_ref[:, off:off + w_diff])
    for j in range(n_blk):
        put_diff(vd32_ref, j, vd[:, j * LANES:(j + 1) * LANES])
    vd16_ref[...] = vd.astype(BF16)

    off += w_diff
    fox = _dot(xb, w_ref[:, off:off + 3 * w_fox])
    qf_ref[...] = (fox[:, 0:w_fox] * q_scale).astype(BF16)
    kf = fox[:, w_fox:2 * w_fox]
    put_fox(kf32_ref, kf)
    kf16_ref[...] = kf.astype(BF16)
    vf = fox[:, 2 * w_fox:3 * w_fox]
    put_fox(vf32_ref, vf)
    vf16_ref[...] = vf.astype(BF16)

    ft = _dot_nt(wft_ref[...], xb)
    logft_ref[...] = jax.nn.log_sigmoid(ft[0:8, :] + bf_ref[...])


def _in_proj(x, w_main, w_ft, b_f, cos_t, sin_t, *, layer, tm, q_scale, stacked=None):
    m, d = x.shape
    n_tab = cos_t.shape[0] // tm
    w_diff, w_fox = d // 2, d // 4
    n_blk = w_diff // LANES
    row = lambda w: pl.BlockSpec((tm, w), lambda i: (i, 0))
    tab = pl.BlockSpec((tm, LANES), lambda i: (i % n_tab, 0))
    sds = jax.ShapeDtypeStruct
    out_shape = [sds((m, w_diff), BF16), sds((m, w_diff), BF16), sds((m, w_diff), BF16),
                 sds((m, w_fox), BF16), sds((m, w_fox), BF16), sds((m, w_fox), BF16), sds((8, m), F32)]
    out_specs = [row(w_diff), row(w_diff), row(w_diff), row(w_fox), row(w_fox), row(w_fox),
                 pl.BlockSpec((8, tm), lambda i: (0, i))]
    in_specs = [row(d), _layer_spec(w_main, layer), _layer_spec(w_ft, layer), _layer_spec(b_f, layer), tab, tab]
    args = [x, w_main, w_ft, b_f, cos_t, sin_t]
    aliases = {}
    scratch = []
    if stacked is None:
        out_shape = [sds((m, d), F32)] + out_shape + [sds((m, w_diff), F32), sds((m, w_diff), F32),
                                                      sds((m, w_fox), F32), sds((m, w_fox), F32)]
        out_specs = [row(d)] + out_specs + [row(w_diff), row(w_diff), row(w_fox), row(w_fox)]
    else:
        halo, wbd, pscale, cw = stacked[:4]
        bufs = list(stacked[4:])
        seq_block = lambda i: (i // n_tab, 0, 0)
        in_specs += [pl.BlockSpec((1,) + halo.shape[1:], seq_block), _layer_spec(wbd, layer),
                     _layer_spec(pscale, layer), _layer_spec(cw, layer)]
        args += [halo, wbd, pscale, cw]
        out_shape = [sds((m, 2 * w_fox), BF16), sds(halo.shape, F32)] + out_shape
        out_specs = [row(2 * w_fox), pl.BlockSpec((1,) + halo.shape[1:], seq_block)] + out_specs
        diff_spec = pl.BlockSpec((1, tm * n_blk, LANES), lambda i: (layer, i, 0))
        fox_spec = pl.BlockSpec((1, 1, w_fox, tm), lambda i: (layer, i // n_tab, 0, i % n_tab))
        aliases = {len(args) + k: len(out_shape) + k for k in range(len(bufs))}
        out_shape += [sds(b.shape, b.dtype) for b in bufs]
        out_specs += [diff_spec, diff_spec, fox_spec, fox_spec]
        in_specs += [pl.BlockSpec(memory_space=pl.ANY)] * len(bufs)
        args += bufs
        scratch = [pltpu.VMEM((HALO_ROWS + tm, 2 * w_fox), F32)]
    return pl.pallas_call(
        functools.partial(_in_proj_body, d=d, q_scale=q_scale, stacked=stacked is not None, n_tab=n_tab),
        out_shape=out_shape,
        grid=(m // tm,),
        in_specs=in_specs,
        out_specs=out_specs,
        scratch_shapes=scratch,
        input_output_aliases=aliases,
        compiler_params=_cparams("arbitrary"),
        name="in_proj",
    )(*args)


def _mixers_body(u_ref, halo_ref, wbd_ref, pscale_ref, cw_ref, yad_ref, state_ref, ext_ref,
                 *, tm, pos0, w):
    out_dtype = yad_ref.dtype
    ti = pl.program_id(1)

    @pl.when(ti == 0)
    def _():
        ext_ref[0:HALO_ROWS, :] = halo_ref[0]

    ya, yd, new_halo = _mix_tile(u_ref[0], ext_ref, wbd_ref[...], pscale_ref[...], cw_ref[...], pos0 + ti * tm)
    yad_ref[0, :, 0:w] = ya.astype(out_dtype)
    yad_ref[0, :, w:2 * w] = yd.astype(out_dtype)
    ext_ref[0:HALO_ROWS, :] = new_halo
    state_ref[0] = new_halo


def _mixers(u_mix, halo, wbd, pscale, cw, *, layer, tm, pos0):
    b, t, d = u_mix.shape
    w = d // 4
    return pl.pallas_call(
        functools.partial(_mixers_body, tm=tm, pos0=pos0, w=w),
        out_shape=[jax.ShapeDtypeStruct((b, t, 2 * w), BF16 if tm % 16 == 0 else F32),
                   jax.ShapeDtypeStruct((b, HALO_ROWS, 2 * w), F32)],
        grid=(b, t // tm),
        in_specs=[pl.BlockSpec((1, tm, d), lambda i, j: (i, j, 0)),
                  pl.BlockSpec((1, HALO_ROWS, 2 * w), lambda i, j: (i, 0, 0)),
                  _layer_spec(wbd, layer), _layer_spec(pscale, layer), _layer_spec(cw, layer)],
        out_specs=[pl.BlockSpec((1, tm, 2 * w), lambda i, j: (i, j, 0)),
                   pl.BlockSpec((1, HALO_ROWS, 2 * w), lambda i, j: (i, 0, 0))],
        scratch_shapes=[pltpu.VMEM((HALO_ROWS + tm, 2 * w), F32)],
        compiler_params=_cparams("parallel", "arbitrary"),
        name="mixers",
    )(u_mix, halo, wbd, pscale, cw)


def _diff_lambda(lqk_ref, lam_init):
    lq = lqk_ref[...]
    a = jnp.sum(lq[0:1, :] * lq[1:2, :], axis=1, keepdims=True)
    b = jnp.sum(lq[2:3, :] * lq[3:4, :], axis=1, keepdims=True)
    return jnp.exp(a) - jnp.exp(b) + lam_init


def _lane_cumsum(x, steps):
    lane = lax.broadcasted_iota(jnp.int32, x.shape, 1)
    s = 1
    for _ in range(steps):
        x = x + jnp.where(lane >= s, pltpu.roll(x, s, 1), 0.0)
        s *= 2
    return x


def _flash_body(*refs, tq, tk, n_split, fox, lam_init):
    if fox:
        q_ref, k_ref, v_ref, frow_ref, fcol_ref, o_ref, vt_ref, fkc_ref, s0_ref, s1_ref, acc_ref = refs
    else:
        q_ref, k_ref, v_ref, lqk_ref, gcol_ref, o_ref, vt_ref, s0_ref, s1_ref, acc_ref = refs
    qi = pl.program_id(2)
    seq = k_ref.shape[0]
    half = HEAD_DIM
    chunk = min(seq, 512)

    @pl.when(qi == 0)
    def _():
        vt_ref[LANES:, :] = jnp.ones((vt_ref.shape[0] - LANES, seq), BF16)
        for c in range(seq // chunk):
            sl = slice(c * chunk, (c + 1) * chunk)
            vt_ref[0:LANES, sl] = v_ref[sl, :].astype(F32).T.astype(BF16)
            if fox:
                fc = fcol_ref[0, 0, sl, :] * LOG2E
                fkc_ref[0, sl, :] = jnp.broadcast_to(fc[:, 0:1], (chunk, LANES))
                fkc_ref[1, sl, :] = jnp.broadcast_to(fc[:, 1:2], (chunk, LANES))

    qt = q_ref[...].astype(F32).T
    low = lax.broadcasted_iota(jnp.int32, (LANES, tq), 0) < half
    qst = jnp.concatenate([jnp.where(low, qt, 0.0), jnp.where(low, 0.0, qt)], axis=1).astype(BF16)

    acc_ref[...] = jnp.zeros(acc_ref.shape, F32)
    q0 = pl.multiple_of(qi * tq, tq)
    cw = tq // n_split
    n_chain = 2 * n_split
    lanes = [slice(c * cw, (c + 1) * cw) for c in range(n_chain)]
    if fox:
        shift = [frow_ref[0, 0, c // n_split:c // n_split + 1, pl.ds(q0 + (c % n_split) * cw, cw)] * LOG2E
                 for c in range(n_chain)]
    n_rep = cw // LANES

    def chain_mode(c, tile):
        g = c % n_split
        return "full" if (tile is None or g > tile) else ("diag" if g == tile else "skip")

    def scores(ki, s_ref, tile=None):
        k = k_ref[pl.ds(pl.multiple_of(ki * tk, tk), tk), :]
        for c in range(n_chain):
            if chain_mode(c, tile) != "skip":
                s_ref[:, lanes[c]] = _dot(k, qst[:, lanes[c]])

    def softmax_pv(ki, s_ref, ms, tile=None):
        start = pl.multiple_of(ki * tk, tk)
        vt = vt_ref[:, pl.ds(start, tk)]
        new = []
        for c in range(n_chain):
            mode = chain_mode(c, tile)
            if mode == "skip":
                new.append(ms[c])
                continue
            st = s_ref[:, lanes[c]]
            if fox:
                st = st - _rep(fkc_ref[c // n_split, pl.ds(start, tk), :], n_rep)
            if mode == "diag":
                visible = (lax.broadcasted_iota(jnp.int32, (tk, cw), 0)
                           <= lax.broadcasted_iota(jnp.int32, (tk, cw), 1))
                st = jnp.where(visible, st, NEG_INF)
            m_cur = jnp.max(st, axis=0, keepdims=True)
            if fox:
                m_cur = m_cur + shift[c]
            m_new = jnp.maximum(ms[c], m_cur)
            alpha = jnp.exp2(ms[c] - m_new)
            p = jnp.exp2(st - ((m_new - shift[c]) if fox else m_new))
            acc_ref[:, lanes[c]] = alpha * acc_ref[:, lanes[c]] + _dot(vt, p.astype(BF16))
            new.append(m_new)
        return tuple(new)

    def two_tiles(j, ms, diagonal):
        scores(2 * j + 1, s1_ref, 1 if diagonal else None)
        ms = softmax_pv(2 * j, s0_ref, ms, 0 if diagonal else None)
        if not diagonal:
            scores(2 * j + 2, s0_ref)
        return softmax_pv(2 * j + 1, s1_ref, ms, 1 if diagonal else None)

    assert cw == tk and tq == 2 * tk
    scores(0, s0_ref)
    ms = lax.fori_loop(0, qi, lambda j, c: two_tiles(j, c, False), (jnp.full((1, cw), NEG_INF, F32),) * n_chain)
    two_tiles(qi, ms, True)

    acc = acc_ref[...]
    out = acc[0:LANES, :] / acc[LANES:LANES + 1, :]
    if fox:
        res = jnp.where(low, out[:, 0:tq], out[:, tq:])
    else:
        lam = _diff_lambda(lqk_ref, lam_init)
        o = out[:, 0:tq] - lam * out[:, tq:]
        ms = jnp.mean(o * o, axis=0, keepdims=True)
        res = o * lax.rsqrt(ms + RMS_EPS) * gcol_ref[...] * (1.0 - lam_init)
    o_ref[...] = res.T.astype(BF16)


def _flash(q, k, v, extra, *, batch, seq, tq, tk, fox, lam_init=0.0):
    m, w = q.shape
    n_blk = w // LANES
    n_q = seq // tq
    q_spec = pl.BlockSpec((tq, LANES), lambda b, h, i: (b * n_q + i, h))
    kv_spec = pl.BlockSpec((seq, LANES), lambda b, h, i: (b, h))
    assert tq == 2 * tk
    ones_rows = 16
    scratch = [pltpu.VMEM((LANES + ones_rows, seq), BF16)]
    if fox:
        extra_specs = [pl.BlockSpec((1, 1, 2, seq), lambda b, h, i: (b, h, 0, 0)),
                       pl.BlockSpec((1, 1, seq, 2), lambda b, h, i: (h, b, 0, 0))]
        scratch.append(pltpu.VMEM((2, seq, LANES), F32))
    else:
        extra_specs = [_const_spec(extra[0].shape), _const_spec(extra[1].shape)]
    scratch += [pltpu.VMEM((tk, 2 * tq), F32), pltpu.VMEM((tk, 2 * tq), F32),
                pltpu.VMEM((LANES + ones_rows, 2 * tq), F32)]
    return pl.pallas_call(
        functools.partial(_flash_body, tq=tq, tk=tk, n_split=max(1, tq // 256), fox=fox, lam_init=lam_init),
        out_shape=jax.ShapeDtypeStruct((m, w), BF16),
        grid=(batch, n_blk, n_q),
        in_specs=[q_spec, kv_spec, kv_spec] + extra_specs,
        out_specs=q_spec,
        scratch_shapes=scratch,
        compiler_params=_cparams("parallel", "parallel", "arbitrary"),
        name="flash_fox" if fox else "flash_diff",
    )(q, k, v, *extra)


def _fcum_prompt_body(lf_ref, o_ref, *, steps):
    o_ref[...] = _lane_cumsum(lf_ref[...], steps)


def _fcum_prompt(logft, *, batch, seq):
    spec = pl.BlockSpec((8, seq), lambda b: (0, b))
    return pl.pallas_call(
        functools.partial(_fcum_prompt_body, steps=(seq - 1).bit_length()),
        out_shape=jax.ShapeDtypeStruct(logft.shape, F32),
        grid=(batch,),
        in_specs=[spec],
        out_specs=spec,
        compiler_params=_cparams("parallel"),
        name="fcum_prompt",
    )(logft)


def _fcum_past_body(pt_ref, *refs, pages):
    in_refs = refs[:pages]
    o_ref, x_ref, carry_ref = refs[pages:]
    depth, _, n_heads, page = in_refs[0].shape
    grp = depth * n_heads
    g = pl.program_id(1)

    @pl.when(g == 0)
    def _():
        carry_ref[...] = jnp.zeros(carry_ref.shape, F32)

    for j in range(pages):
        for d in range(depth):
            x_ref[pl.ds(j * grp + d * n_heads, n_heads), :] = in_refs[j][d, 0]
    incl = _lane_cumsum(x_ref[...], int(math.log2(page)))
    tot = jnp.broadcast_to(incl[:, page - 1:page], incl.shape)
    row = lax.broadcasted_iota(jnp.int32, incl.shape, 0)
    pre = tot
    s = grp
    while s < pages * grp:
        pre = pre + jnp.where(row >= s, pltpu.roll(pre, s, 0), 0.0)
        s *= 2
    carry = carry_ref[...]
    x_ref[...] = incl + (pre - tot) + jnp.concatenate([carry] * pages, axis=0)
    carry_ref[...] = carry + pre[(pages - 1) * grp:, :]
    for j in range(pages):
        for d in range(depth):
            o_ref[d, 0, j] = x_ref[pl.ds(j * grp + d * n_heads, n_heads), :]


def _fcum_past(logf_t, page_table, *, pages):
    depth, n_pool, n_heads, page = logf_t.shape
    n_seq, n_pages = page_table.shape
    in_specs = [pl.BlockSpec((depth, 1, n_heads, page),
                             functools.partial(lambda b, g, pt, j: (0, pt[b, g * pages + j], 0, 0), j=j))
                for j in range(pages)]
    grid_spec = pltpu.PrefetchScalarGridSpec(
        num_scalar_prefetch=1,
        grid=(n_seq, n_pages // pages),
        in_specs=in_specs,
        out_specs=pl.BlockSpec((depth, 1, pages, n_heads, page), lambda b, g, pt: (0, b, g, 0, 0)),
        scratch_shapes=[pltpu.VMEM((pages * depth * n_heads, page), F32),
                        pltpu.VMEM((depth * n_heads, page), F32)])
    return pl.pallas_call(
        functools.partial(_fcum_past_body, pages=pages),
        out_shape=jax.ShapeDtypeStruct((depth, n_seq, n_pages, n_heads, page), F32),
        grid_spec=grid_spec,
        compiler_params=_cparams("parallel", "arbitrary"),
        name="fcum_past",
    )(page_table, *([logf_t] * pages))


def _softmax_step(s, pv, m_ref, l_ref, acc_ref, shift=None):
    n_k = s.shape[1] // LANES
    n_v = acc_ref.shape[1] // LANES
    m_prev = m_ref[...]
    m_cur = jnp.max(s, axis=1, keepdims=True)
    if shift is not None:
        m_cur = m_cur + shift
    m_new = jnp.maximum(m_prev, m_cur)
    alpha = jnp.exp(m_prev - m_new)
    sub = m_new if shift is None else m_new - shift
    p = jnp.exp(s - _rep(sub, n_k))
    l_ref[...] = alpha * l_ref[...] + jnp.sum(p, axis=1, keepdims=True)
    acc_ref[...] = _rep(alpha, n_v) * acc_ref[...] + pv(p.astype(BF16))
    m_ref[...] = m_new


def _pad_rows(x, rows):
    return jnp.concatenate([x, jnp.zeros((rows - x.shape[0], x.shape[1]), x.dtype)], axis=0)


def _paged_copies(pt_ref, caches, bufs, sems, *, layer, pages, seq, step, slot):
    return [pltpu.make_async_copy(cache.at[layer, pt_ref[seq, step * pages + j]], buf.at[slot, j],
                                  sems.at[a, slot])
            for a, (cache, buf) in enumerate(zip(caches, bufs)) for j in range(pages)]


def _when(cond, fn):
    if isinstance(cond, bool):
        if cond:
            fn()
    else:
        pl.when(cond)(fn)


def _paged_pipeline(pt_ref, caches, bufs, sems, *, layer, pages, seq, step, n_seq, n_steps):
    s = seq * n_steps + step
    slot = lax.rem(s, 2)
    copies = functools.partial(_paged_copies, pt_ref, caches, bufs, sems, layer=layer, pages=pages)

    def start_own():
        for cp in copies(seq=seq, step=step, slot=slot):
            cp.start()

    def start_next():
        wrap = step + 1 == n_steps
        for cp in copies(seq=jnp.where(wrap, seq + 1, seq), step=jnp.where(wrap, 0, step + 1), slot=1 - slot):
            cp.start()

    _when(s == 0, start_own)
    _when(s + 1 < n_seq * n_steps, start_next)
    for cp in copies(seq=seq, step=step, slot=slot):
        cp.wait()
    return slot


def _decode_diff_body(pt_ref, q_ref, knew_ref, vnew_ref, lqk_ref, g_ref, k_hbm, v_hbm, o_ref,
                      kbuf_ref, vbuf_ref, sem_ref, qs_ref, m_ref, l_ref, acc_ref, *, layer, pages, **static):
    g, n_g = pl.program_id(1), pl.num_programs(1)
    slot = _paged_pipeline(pt_ref, (k_hbm, v_hbm), (kbuf_ref, vbuf_ref), sem_ref, layer=layer, pages=pages,
                           seq=pl.program_id(0), step=g, n_seq=pl.num_programs(0), n_steps=n_g)
    _decode_diff_step(q_ref, knew_ref, vnew_ref, lqk_ref, g_ref, o_ref, kbuf_ref, vbuf_ref,
                      qs_ref, m_ref, l_ref, acc_ref, slot=slot, first=g == 0, last=g == n_g - 1,
                      pages=pages, **static)


def _decode_diff_step(q_ref, knew_ref, vnew_ref, lqk_ref, g_ref, o_ref, kbuf_ref, vbuf_ref,
                      qs_ref, m_ref, l_ref, acc_ref, *, slot, first, last, pages, n_heads, t_new, lam_init,
                      between=None):
    page = kbuf_ref.shape[2] // n_heads
    hw = 2 * HEAD_DIM
    rows_h = 2 * t_new
    rows_p = 2 * rows_h

    def init():
        q = q_ref[0]
        low = lax.broadcasted_iota(jnp.int32, (t_new, hw), 1) < HEAD_DIM
        zero = jnp.zeros((t_new, hw), F32)
        pieces = []
        for h in range(n_heads):
            qh = q[:, h * hw:(h + 1) * hw]
            for piece in (jnp.where(low, qh, 0.0), jnp.where(low, 0.0, qh)):
                pieces.append(jnp.concatenate([piece, zero] if h % 2 == 0 else [zero, piece], axis=1))
        qs_ref[...] = jnp.concatenate(pieces, axis=0).astype(BF16)
        m_ref[...] = jnp.full(m_ref.shape, NEG_INF, F32)
        l_ref[...] = jnp.zeros(l_ref.shape, F32)
        acc_ref[...] = jnp.zeros(acc_ref.shape, F32)

    _when(first, init)

    def head_rows(buf_ref, h):
        return jnp.concatenate([buf_ref[slot, j, pl.ds(h, page, stride=n_heads), :].astype(BF16)
                                for j in range(pages)], axis=0)

    def pair_cols(buf_ref, hp):
        return jnp.concatenate([head_rows(buf_ref, 2 * hp), head_rows(buf_ref, 2 * hp + 1)], axis=1)

    def per_pair(fn):
        return jnp.concatenate([fn(hp, slice(hp * rows_p, (hp + 1) * rows_p)) for hp in range(n_heads // 2)],
                               axis=0)

    def own_block(x):
        first_head = lax.broadcasted_iota(jnp.int32, (rows_p, hw), 0) < rows_h
        return jnp.where(first_head, x[:, 0:hw], x[:, hw:])

    qs = qs_ref[...]
    vs = [pair_cols(vbuf_ref, hp) for hp in range(n_heads // 2)]
    s = per_pair(lambda hp, rows: _dot_nt(qs[rows, :], pair_cols(kbuf_ref, hp)))
    if between is not None:
        between()
    _softmax_step(s, lambda p: per_pair(lambda hp, rows: own_block(_dot(p[rows, :], vs[hp]))),
                  m_ref, l_ref, acc_ref)

    def finish():
        k_new = knew_ref[0]
        v_new = vnew_ref[0]
        pair = lambda x, hp: _pad_rows(x[:, hp * 2 * hw:(hp + 1) * 2 * hw], LANES).astype(BF16)
        s_new = per_pair(lambda hp, rows: _dot_nt(qs[rows, :], pair(k_new, hp)))
        row = lax.broadcasted_iota(jnp.int32, s_new.shape, 0) & (t_new - 1)
        col = lax.broadcasted_iota(jnp.int32, s_new.shape, 1)
        s_new = jnp.where(col <= row, s_new, NEG_INF)
        _softmax_step(s_new, lambda p: per_pair(lambda hp, rows: own_block(_dot(p[rows, :], pair(v_new, hp)))),
                      m_ref, l_ref, acc_ref)

        out = acc_ref[...] / l_ref[...]
        lam = _diff_lambda(lqk_ref, lam_init)
        for h in range(n_heads):
            o = out[h * rows_h:h * rows_h + t_new, :] - lam * out[h * rows_h + t_new:(h + 1) * rows_h, :]
            ms = jnp.mean(o * o, axis=-1, keepdims=True)
            o_ref[0, :, h * hw:(h + 1) * hw] = o * lax.rsqrt(ms + RMS_EPS) * g_ref[...] * (1.0 - lam_init)

    _when(last, finish)


def _decode_diff(q, k_new, v_new, cache_k, cache_v, page_table, lqk, g, *, layer, pages, lam_init):
    n_seq, t_new, width = q.shape
    n_heads = width // (2 * HEAD_DIM)
    n_pages = page_table.shape[1]
    rows = cache_k.shape[2]
    seq_spec = pl.BlockSpec((1, t_new, width), lambda b, g_, pt: (b, 0, 0))
    hbm = pl.BlockSpec(memory_space=pl.ANY)
    n_rows = n_heads * 2 * t_new
    grid_spec = pltpu.PrefetchScalarGridSpec(
        num_scalar_prefetch=1,
        grid=(n_seq, n_pages // pages),
        in_specs=[seq_spec, seq_spec, seq_spec,
                  pl.BlockSpec(lqk.shape, lambda b, g_, pt: (0, 0)),
                  pl.BlockSpec(g.shape, lambda b, g_, pt: (0, 0)), hbm, hbm],
        out_specs=seq_spec,
        scratch_shapes=[pltpu.VMEM((2, pages, rows, LANES), F32), pltpu.VMEM((2, pages, rows, LANES), F32),
                        pltpu.SemaphoreType.DMA((2, 2)),
                        pltpu.VMEM((n_rows, 2 * LANES), BF16), pltpu.VMEM((n_rows, LANES), F32),
                        pltpu.VMEM((n_rows, LANES), F32), pltpu.VMEM((n_rows, LANES), F32)])
    return pl.pallas_call(
        functools.partial(_decode_diff_body, layer=layer, pages=pages, n_heads=n_heads, t_new=t_new,
                          lam_init=lam_init),
        out_shape=jax.ShapeDtypeStruct((n_seq, t_new, width), F32),
        grid_spec=grid_spec,
        compiler_params=_cparams("arbitrary", "arbitrary"),
        name="decode_diff",
    )(page_table, q, k_new, v_new, lqk, g, cache_k, cache_v)


def _ffn_decode_diff_body(pt_ref, x_ref, wg_ref, wu_ref, wd_ref, lng_ref, lnb_ref,
                          q_ref, knew_ref, vnew_ref, lqk_ref, g_ref, k_hbm, v_hbm, o_ref, yb_ref,
                          acc_ref, kbuf_ref, vbuf_ref, sem_ref, qs_ref, m_ref, l_ref, dacc_ref,
                          *, chunk, alpha, layer, pages, n_steps, **static):
    i = pl.program_id(0)
    x = x_ref[...]
    xb = x.astype(BF16)
    n_chunks = wg_ref.shape[1] // chunk
    per = -(-n_chunks // n_steps)
    for g in range(n_steps):
        slot = _paged_pipeline(pt_ref, (k_hbm, v_hbm), (kbuf_ref, vbuf_ref), sem_ref, layer=layer, pages=pages,
                               seq=i, step=g, n_seq=pl.num_programs(0), n_steps=n_steps)
        mine = list(range(g * per, min((g + 1) * per, n_chunks)))
        k = max(1, len(mine) // 3)
        ffn = lambda cs: [_ffn_chunk(xb, wg_ref, wu_ref, wd_ref, acc_ref, c, chunk) for c in cs]
        ffn(mine[:k])
        _decode_diff_step(q_ref, knew_ref, vnew_ref, lqk_ref, g_ref, yb_ref, kbuf_ref, vbuf_ref,
                          qs_ref, m_ref, l_ref, dacc_ref, slot=slot, first=g == 0, last=g == n_steps - 1,
                          pages=pages, between=lambda: ffn(mine[k:2 * k]), **static)
        ffn(mine[2 * k:])
    z = alpha * x + 0.5 * acc_ref[...]
    o_ref[...] = _layer_norm(z, lng_ref[...], lnb_ref[...])


def _ffn_ln_decode_diff(x, wg, wu, wd, g, b, q, k_new, v_new, cache_k, cache_v, page_table, lqk, sg,
                        *, layer, tm, alpha, pages, lam_init):
    m, d = x.shape
    n_seq, t_new, width = q.shape
    assert m // tm == n_seq
    n_heads = width // (2 * HEAD_DIM)
    n_steps = page_table.shape[1] // pages
    rows = cache_k.shape[2]
    row = pl.BlockSpec((tm, d), lambda i, pt: (i, 0))
    seq_spec = pl.BlockSpec((1, t_new, width), lambda i, pt: (i, 0, 0))
    hbm = pl.BlockSpec(memory_space=pl.ANY)
    n_rows = n_heads * 2 * t_new
    grid_spec = pltpu.PrefetchScalarGridSpec(
        num_scalar_prefetch=1,
        grid=(n_seq,),
        in_specs=[row, _layer_spec(wg, layer), _layer_spec(wu, layer), _layer_spec(wd, layer),
                  _const_spec((1, d)), _const_spec((1, d)), seq_spec, seq_spec, seq_spec,
                  _const_spec(lqk.shape), _const_spec(sg.shape), hbm, hbm],
        out_specs=[row, seq_spec],
        scratch_shapes=[pltpu.VMEM((tm, d), F32),
                        pltpu.VMEM((2, pages, rows, LANES), F32), pltpu.VMEM((2, pages, rows, LANES), F32),
                        pltpu.SemaphoreType.DMA((2, 2)),
                        pltpu.VMEM((n_rows, 2 * LANES), BF16), pltpu.VMEM((n_rows, LANES), F32),
                        pltpu.VMEM((n_rows, LANES), F32), pltpu.VMEM((n_rows, LANES), F32)])
    return pl.pallas_call(
        functools.partial(_ffn_decode_diff_body, chunk=FFN_CHUNK, alpha=alpha, layer=layer, pages=pages,
                          n_steps=n_steps, n_heads=n_heads, t_new=t_new, lam_init=lam_init),
        out_shape=[jax.ShapeDtypeStruct((m, d), F32), jax.ShapeDtypeStruct((n_seq, t_new, width), F32)],
        grid_spec=grid_spec,
        compiler_params=_cparams("arbitrary"),
        name="ffn_ln_decode_diff",
    )(page_table, x, wg, wu, wd, g, b, q, k_new, v_new, lqk, sg, cache_k, cache_v)


def _decode_fox_body(pt_ref, q_ref, knew_ref, vnew_ref, lfnew_ref, fk_ref, kt_hbm, vt_hbm, o_ref,
                     ktbuf_ref, vtbuf_ref, sem_ref, qs_ref, m_ref, l_ref, acc_ref, *, layer, pages, **static):
    g, n_g = pl.program_id(1), pl.num_programs(1)
    slot = _paged_pipeline(pt_ref, (kt_hbm, vt_hbm), (ktbuf_ref, vtbuf_ref), sem_ref, layer=layer, pages=pages,
                           seq=pl.program_id(0), step=g, n_seq=pl.num_programs(0), n_steps=n_g)
    _decode_fox_step(q_ref, knew_ref, vnew_ref, lfnew_ref, fk_ref, o_ref, ktbuf_ref, vtbuf_ref,
                     qs_ref, m_ref, l_ref, acc_ref, slot=slot, first=g == 0, last=g == n_g - 1,
                     page_base=g * pages, pages=pages, **static)


def _decode_fox_step(q_ref, knew_ref, vnew_ref, lfnew_ref, fk_ref, o_ref, ktbuf_ref, vtbuf_ref,
                     qs_ref, m_ref, l_ref, acc_ref, *, slot, first, last, page_base, pages, n_heads, t_new,
                     between=None):
    width = q_ref.shape[2]
    page = ktbuf_ref.shape[3]
    n_rows = n_heads * t_new

    def init():
        q = q_ref[0]
        lane = lax.broadcasted_iota(jnp.int32, (t_new, width), 1)
        qs_ref[...] = jnp.concatenate(
            [jnp.where((lane >= h * HEAD_DIM) & (lane < (h + 1) * HEAD_DIM), q, 0.0) for h in range(n_heads)],
            axis=0).astype(BF16)
        m_ref[...] = jnp.full(m_ref.shape, NEG_INF, F32)
        l_ref[...] = jnp.zeros(l_ref.shape, F32)
        acc_ref[...] = jnp.zeros(acc_ref.shape, F32)

    _when(first, init)

    def head_rows(per_head):
        return jnp.concatenate([jnp.broadcast_to(per_head(h), (t_new, per_head(h).shape[1]))
                                for h in range(n_heads)], axis=0)

    total = fk_ref[0, 0, fk_ref.shape[2] - 1, :, page - 1:page]
    f_new = _lane_cumsum(lfnew_ref[0], int(math.log2(t_new))) + jnp.concatenate(
        [total, jnp.zeros((8 - n_heads, 1), F32)], axis=0)
    eye = (lax.broadcasted_iota(jnp.int32, (t_new, LANES), 0)
           == lax.broadcasted_iota(jnp.int32, (t_new, LANES), 1))
    fq_col = jnp.concatenate(
        [jnp.sum(jnp.where(eye, jnp.broadcast_to(f_new[h:h + 1, :], (t_new, LANES)), 0.0),
                 axis=1, keepdims=True) for h in range(n_heads)], axis=0)
    shift = jnp.broadcast_to(fq_col, (n_rows, LANES))

    n_chain = m_ref.shape[0]
    per = pages // n_chain
    qs = qs_ref[...]
    scores = []
    for c in range(n_chain):
        sel = range(c * per, (c + 1) * per)
        kt = jnp.concatenate([ktbuf_ref[slot, j].astype(BF16) for j in sel], axis=1)
        fk = jnp.concatenate([head_rows(lambda h: fk_ref[0, 0, page_base + j, h:h + 1, :]) for j in sel], axis=1)
        scores.append(_dot(qs, kt) - fk)
    if between is not None:
        between()
    for c in range(n_chain):
        vt = jnp.concatenate([vtbuf_ref[slot, j].astype(BF16) for j in range(c * per, (c + 1) * per)], axis=1)
        _softmax_step(scores[c], lambda p, vt=vt: _dot_nt(p, vt), m_ref.at[c], l_ref.at[c], acc_ref.at[c], shift)

    def finish():
        k_new = _pad_rows(knew_ref[0], LANES).astype(BF16)
        v_new = _pad_rows(vnew_ref[0], LANES).astype(BF16)
        s_new = _dot_nt(qs, k_new) - head_rows(lambda h: f_new[h:h + 1, :])
        row = lax.broadcasted_iota(jnp.int32, s_new.shape, 0) & (t_new - 1)
        col = lax.broadcasted_iota(jnp.int32, s_new.shape, 1)
        s_new = jnp.where(col <= row, s_new, NEG_INF)
        _softmax_step(s_new, lambda p: _dot(p, v_new), m_ref.at[0], l_ref.at[0], acc_ref.at[0], shift)

        m_all = m_ref[0]
        for c in range(1, n_chain):
            m_all = jnp.maximum(m_all, m_ref[c])
        l_all = jnp.zeros(m_all.shape, F32)
        acc_all = jnp.zeros(acc_ref.shape[1:], F32)
        for c in range(n_chain):
            wgt = jnp.exp(m_ref[c] - m_all)
            l_all = l_all + wgt * l_ref[c]
            acc_all = acc_all + _rep(wgt, width // LANES) * acc_ref[c]
        out = acc_all / _rep(l_all, width // LANES)
        lane = lax.broadcasted_iota(jnp.int32, (t_new, width), 1)
        res = jnp.zeros((t_new, width), F32)
        for h in range(n_heads):
            keep = (lane >= h * HEAD_DIM) & (lane < (h + 1) * HEAD_DIM)
            res = jnp.where(keep, out[h * t_new:(h + 1) * t_new, :], res)
        o_ref[0] = res

    _when(last, finish)


def _decode_fox_pieces(q, cache_kt, page_table, fk, *, layer, pages, seq_index):
    n_seq, t_new, width = q.shape
    n_heads = width // HEAD_DIM
    page = cache_kt.shape[3]
    n_pages = page_table.shape[1]
    seq_spec = pl.BlockSpec((1, t_new, width), lambda *a: (seq_index(*a), 0, 0))
    hbm = pl.BlockSpec(memory_space=pl.ANY)
    n_rows = n_heads * t_new
    n_chain = 2 if pages % 2 == 0 else 1
    in_specs = [seq_spec, seq_spec, seq_spec,
                pl.BlockSpec((1, 8, LANES), lambda *a: (seq_index(*a), 0, 0)),
                pl.BlockSpec((1, 1, n_pages, n_heads, page), lambda *a: (layer, seq_index(*a), 0, 0, 0)),
                hbm, hbm]
    scratch = [pltpu.VMEM((2, pages, width, page), F32), pltpu.VMEM((2, pages, width, page), F32),
               pltpu.SemaphoreType.DMA((2, 2)),
               pltpu.VMEM((n_rows, width), BF16), pltpu.VMEM((n_chain, n_rows, LANES), F32),
               pltpu.VMEM((n_chain, n_rows, LANES), F32), pltpu.VMEM((n_chain, n_rows, width), F32)]
    return in_specs, seq_spec, scratch, dict(n_heads=n_heads, t_new=t_new)


def _decode_fox(q, k_new, v_new, cache_kt, cache_vt, page_table, lf_new, fk, *, layer, pages):
    n_seq, t_new, width = q.shape
    in_specs, seq_spec, scratch, static = _decode_fox_pieces(
        q, cache_kt, page_table, fk, layer=layer, pages=pages, seq_index=lambda b, g, pt: b)
    grid_spec = pltpu.PrefetchScalarGridSpec(
        num_scalar_prefetch=1, grid=(n_seq, page_table.shape[1] // pages),
        in_specs=in_specs, out_specs=seq_spec, scratch_shapes=scratch)
    return pl.pallas_call(
        functools.partial(_decode_fox_body, layer=layer, pages=pages, **static),
        out_shape=jax.ShapeDtypeStruct((n_seq, t_new, width), F32),
        grid_spec=grid_spec,
        compiler_params=_cparams("arbitrary", "arbitrary"),
        name="decode_fox",
    )(page_table, q, k_new, v_new, lf_new, fk, cache_kt, cache_vt)


def _ffn_decode_fox_body(pt_ref, x_ref, wg_ref, wu_ref, wd_ref, lng_ref, lnb_ref,
                         q_ref, knew_ref, vnew_ref, lfnew_ref, fk_ref, kt_hbm, vt_hbm, o_ref, yc_ref,
                         acc_ref, ktbuf_ref, vtbuf_ref, sem_ref, qs_ref, m_ref, l_ref, dacc_ref,
                         *, chunk, alpha, layer, pages, n_steps, **static):
    i = pl.program_id(0)
    x = x_ref[...]
    xb = x.astype(BF16)
    n_chunks = wg_ref.shape[1] // chunk
    per = -(-n_chunks // n_steps)
    for g in range(n_steps):
        slot = _paged_pipeline(pt_ref, (kt_hbm, vt_hbm), (ktbuf_ref, vtbuf_ref), sem_ref, layer=layer,
                               pages=pages, seq=i, step=g, n_seq=pl.num_programs(0), n_steps=n_steps)
        mine = list(range(g * per, min((g + 1) * per, n_chunks)))
        k = max(1, len(mine) // 3)
        ffn = lambda cs: [_ffn_chunk(xb, wg_ref, wu_ref, wd_ref, acc_ref, c, chunk) for c in cs]
        ffn(mine[:k])
        _decode_fox_step(q_ref, knew_ref, vnew_ref, lfnew_ref, fk_ref, yc_ref, ktbuf_ref, vtbuf_ref,
                         qs_ref, m_ref, l_ref, dacc_ref, slot=slot, first=g == 0, last=g == n_steps - 1,
                         page_base=g * pages, pages=pages, between=lambda: ffn(mine[k:2 * k]), **static)
        ffn(mine[2 * k:])
    z = alpha * x + 0.5 * acc_ref[...]
    o_ref[...] = _layer_norm(z, lng_ref[...], lnb_ref[...])


def _ffn_ln_decode_fox(x, wg, wu, wd, g, b, q, k_new, v_new, cache_kt, cache_vt, page_table, lf_new, fk,
                       *, layer, tm, alpha, pages):
    m, d = x.shape
    n_seq, t_new, width = q.shape
    assert m // tm == n_seq
    dec_specs, seq_spec, dec_scratch, static = _decode_fox_pieces(
        q, cache_kt, page_table, fk, layer=layer, pages=pages, seq_index=lambda i, pt: i)
    row = pl.BlockSpec((tm, d), lambda i, pt: (i, 0))
    grid_spec = pltpu.PrefetchScalarGridSpec(
        num_scalar_prefetch=1,
        grid=(n_seq,),
        in_specs=[row, _layer_spec(wg, layer), _layer_spec(wu, layer), _layer_spec(wd, layer),
                  _const_spec((1, d)), _const_spec((1, d))] + dec_specs,
        out_specs=[row, seq_spec],
        scratch_shapes=[pltpu.VMEM((tm, d), F32)] + dec_scratch)
    return pl.pallas_call(
        functools.partial(_ffn_decode_fox_body, chunk=FFN_CHUNK, alpha=alpha, layer=layer, pages=pages,
                          n_steps=page_table.shape[1] // pages, **static),
        out_shape=[jax.ShapeDtypeStruct((m, d), F32), jax.ShapeDtypeStruct((n_seq, t_new, width), F32)],
        grid_spec=grid_spec,
        compiler_params=_cparams("arbitrary"),
        name="ffn_ln_decode_fox",
    )(page_table, x, wg, wu, wd, g, b, q, k_new, v_new, lf_new, fk, cache_kt, cache_vt)


def _merge_body(x_ref, yad_ref, yb_ref, yc_ref, wg_ref, wp_ref, wo_ref, g_ref, b_ref, o_ref, *, alpha):
    x = x_ref[...]
    xb = x.astype(BF16)
    d = x.shape[1]
    w = d // 4
    yad = yad_ref[...]
    ys = (yad[:, 0:w], yb_ref[...], yc_ref[...], yad[:, w:2 * w])
    merged = None
    row = 0
    for i, y in enumerate(ys):
        gate = jax.nn.sigmoid(_dot(xb, wg_ref[:, i * d:(i + 1) * d]))
        term = gate * _dot(y.astype(BF16), wp_ref[row:row + y.shape[1], :])
        merged = term if merged is None else merged + term
        row += y.shape[1]
    out = _dot(merged.astype(BF16), wo_ref[...])
    o_ref[...] = _layer_norm(alpha * x + out, g_ref[...], b_ref[...])


def _merge(x, yad, yb, yc, wg, wp, wo, g, b, *, layer, tm, alpha):
    m, d = x.shape
    row = lambda w: pl.BlockSpec((tm, w), lambda i: (i, 0))
    return pl.pallas_call(
        functools.partial(_merge_body, alpha=alpha),
        out_shape=jax.ShapeDtypeStruct((m, d), F32),
        grid=(m // tm,),
        in_specs=[row(d), row(yad.shape[1]), row(yb.shape[1]), row(yc.shape[1]),
                  _layer_spec(wg, layer), _layer_spec(wp, layer), _layer_spec(wo, layer),
                  _const_spec((1, d)), _const_spec((1, d))],
        out_specs=row(d),
        compiler_params=_cparams("parallel"),
        name="merge",
    )(x, yad, yb, yc, wg, wp, wo, g, b)


def _rope_tables(pos):
    half = HEAD_DIM // 2
    inv = ROPE_THETA ** (-jnp.arange(half, dtype=F32) / half)
    ang = pos.astype(F32)[:, None] * inv[None, :]
    cos, sin = jnp.cos(ang), jnp.sin(ang)
    cos_t = jnp.tile(cos, (1, LANES // half))
    sin_t = jnp.tile(jnp.concatenate([-sin, sin], axis=1), (1, LANES // HEAD_DIM))
    return cos_t, sin_t


def _pick_tile(n, target):
    t = min(n, target)
    while n % t:
        t //= 2
    return t


def kernel(x_prompt, x_sample, cache_diff_k, cache_diff_v, cache_fox_k, cache_fox_v, cache_fox_logf,
           state_pool, state_conv, page_table, w_in, b_fgate, pool_w, pool_scale, lambda_qk, subln_g,
           conv_w, w_branch, w_o, w_ffn1_gate, w_ffn1_up, w_ffn1_down, w_ffn2_gate, w_ffn2_up,
           w_ffn2_down, ln_g, ln_b):
    batch, seq, d = x_prompt.shape
    n_seq, t_new, _ = x_sample.shape
    depth = w_in.shape[0]
    n_pool, page = cache_diff_k.shape[1:3]
    past_len = page_table.shape[1] * page
    h_diff = cache_diff_k.shape[3]
    h_fox = cache_fox_k.shape[3]
    w_pool = state_pool.shape[-1]
    w_diff, w_fox, w_conv = d // 2, d // 4, d // 4
    alpha = (2.0 * depth) ** 0.25

    sizes = (w_pool, w_diff, w_diff, w_diff, w_fox, w_fox, w_fox, h_fox, w_conv, w_conv, w_conv, 4 * d)
    offs = np.concatenate([[0], np.cumsum(sizes)])
    col = lambda i: w_in[:, :, offs[i]:offs[i + 1]]
    w_main = jnp.concatenate([col(0), col(8), col(9), col(10), col(1), col(2), col(3), col(4), col(5), col(6)],
                             axis=-1).astype(BF16)
    w_ft = jnp.pad(jnp.swapaxes(col(7), 1, 2), ((0, 0), (0, 16 - h_fox), (0, 0))).astype(BF16)
    b_f = jnp.pad(b_fgate, ((0, 0), (0, 8 - h_fox)))[:, :, None]
    w_gate = col(11).astype(BF16)
    w_br = w_branch.astype(BF16)
    w_out = w_o.astype(BF16)
    ffn = [tuple(w.astype(BF16) for w in ws) for ws in
           ((w_ffn1_gate, w_ffn1_up, w_ffn1_down), (w_ffn2_gate, w_ffn2_up, w_ffn2_down))]
    n_grp, grp = pool_w.shape[1], pool_w.shape[2]
    wbd = jnp.zeros((depth, w_pool, w_pool), F32)
    for gidx in range(n_grp):
        wbd = wbd.at[:, gidx * grp:(gidx + 1) * grp, gidx * grp:(gidx + 1) * grp].set(pool_w[:, gidx])
    wbd = wbd.astype(BF16)
    pscale = pool_scale[:, None, :]

    cos_p, sin_p = _rope_tables(jnp.arange(seq, dtype=jnp.int32))
    cos_s, sin_s = _rope_tables(past_len + jnp.arange(t_new, dtype=jnp.int32))
    cos_s, sin_s = jnp.tile(cos_s, (n_seq, 1)), jnp.tile(sin_s, (n_seq, 1))

    ck_diff = cache_diff_k.reshape(depth, n_pool, page * h_diff, 2 * HEAD_DIM)
    cv_diff = cache_diff_v.reshape(depth, n_pool, page * h_diff, 2 * HEAD_DIM)
    ck_fox = jnp.transpose(cache_fox_k, (0, 1, 3, 4, 2)).reshape(depth, n_pool, h_fox * HEAD_DIM, page)
    cv_fox = jnp.transpose(cache_fox_v, (0, 1, 3, 4, 2)).reshape(depth, n_pool, h_fox * HEAD_DIM, page)
    pages = _pick_tile(page_table.shape[1], 16)
    fk_past = _fcum_past(jnp.transpose(cache_fox_logf, (0, 1, 3, 2)), page_table, pages=pages)

    halo_p = jnp.zeros((batch, HALO_ROWS, 2 * w_pool), F32)

    tm_p = _pick_tile(batch * seq, 512)
    tm_s = n_seq * t_new
    d_ff = w_ffn1_gate.shape[2]
    tf_s = d_ff // 2 if (d_ff // 2) % LANES == 0 else d_ff
    tq = _pick_tile(seq, 512)
    tk = tq // 2

    pages_fox = _pick_tile(page_table.shape[1], 32)
    p_kv = [jnp.zeros((depth, batch * seq * h_diff, 2 * HEAD_DIM), F32) for _ in range(2)]
    p_kv += [jnp.zeros((depth, batch, h_fox * HEAD_DIM, seq), F32) for _ in range(2)]

    xp = x_prompt.reshape(batch * seq, d)
    xs = x_sample.reshape(n_seq * t_new, d)
    p_rows, s_rows = [], []
    for l in range(depth):
        lam_init = 0.8 - 0.6 * math.exp(-0.3 * l)
        lng = lambda i: ln_g[l, i][None, :]
        lnb = lambda i: ln_b[l, i][None, :]
        halo_s = jnp.concatenate(
            [jnp.pad(state_pool[l], ((0, 0), (HALO_ROWS - state_pool.shape[2], 0), (0, 0))),
             jnp.pad(state_conv[l], ((0, 0), (HALO_ROWS - state_conv.shape[2], 0), (0, 0)))], axis=-1)

        xs = _ffn_ln_stream(xs, *ffn[0], lng(0), lnb(0), layer=l, tm=tm_s, tf=tf_s, alpha=alpha)
        u_mix, s_qd, _, _, s_qf, _, _, s_logft, s_kd, s_vd, s_kf, s_vf = _in_proj(
            xs, w_main, w_ft, b_f, cos_s, sin_s, layer=l, tm=tm_s, q_scale=1.0 / math.sqrt(HEAD_DIM))
        s_yad, s_state = _mixers(u_mix.reshape(n_seq, t_new, d), halo_s, wbd, pscale, conv_w,
                                 layer=l, tm=t_new, pos0=past_len)
        seq3 = lambda a: a.reshape(n_seq, t_new, a.shape[-1])
        diff_args = (seq3(s_qd.astype(F32)), seq3(s_kd), seq3(s_vd), ck_diff, cv_diff, page_table,
                     lambda_qk[l], subln_g[l][None, :])

        if (batch * seq) // tm_p == n_seq:
            xp, s_yb = _ffn_ln_decode_diff(xp, *ffn[0], lng(0), lnb(0), *diff_args, layer=l, tm=tm_p,
                                           alpha=alpha, pages=pages, lam_init=lam_init)
        else:
            xp = _ffn_ln(xp, *ffn[0], lng(0), lnb(0), layer=l, tm=tm_p, alpha=alpha)
            s_yb = _decode_diff(*diff_args, layer=l, pages=pages, lam_init=lam_init)
        yad, state, qd, kd16, vd16, qf, kf16, vf16, logft, *p_kv = _in_proj(
            xp, w_main, w_ft, b_f, cos_p, sin_p, layer=l, tm=tm_p,
            q_scale=LOG2E / math.sqrt(HEAD_DIM),
            stacked=(halo_p, wbd, pscale, conv_w, *p_kv))
        yb = _flash(qd, kd16, vd16, (lambda_qk[l], subln_g[l][:, None]), batch=batch, seq=seq, tq=tq, tk=tk,
                    fox=False, lam_init=lam_init)
        logf = logft[:h_fox].reshape(h_fox, batch, seq)
        f_cum = _fcum_prompt(logft, batch=batch, seq=seq)[:h_fox].reshape(h_fox // 2, 2, batch, seq)
        f_row = jnp.transpose(f_cum, (2, 0, 1, 3))
        f_col = jnp.transpose(f_cum, (0, 2, 3, 1))
        yc = _flash(qf, kf16, vf16, (f_row, f_col), batch=batch, seq=seq, tq=tq, tk=tk, fox=True)
        xp = _merge(xp, yad, yb, yc, w_gate, w_br, w_out, lng(1), lnb(1), layer=l, tm=tm_p, alpha=alpha)
        s_logf = s_logft[:h_fox].reshape(h_fox, n_seq, t_new)
        lf_new = jnp.pad(jnp.transpose(s_logf, (1, 0, 2)), ((0, 0), (0, 8 - h_fox), (0, LANES - t_new)))
        fox_args = (seq3(s_qf.astype(F32)), seq3(s_kf), seq3(s_vf), ck_fox, cv_fox, page_table, lf_new, fk_past)
        if (batch * seq) // tm_p == n_seq:
            xp, s_yc = _ffn_ln_decode_fox(xp, *ffn[1], lng(2), lnb(2), *fox_args, layer=l, tm=tm_p,
                                          alpha=alpha, pages=pages_fox)
        else:
            xp = _ffn_ln(xp, *ffn[1], lng(2), lnb(2), layer=l, tm=tm_p, alpha=alpha)
            s_yc = _decode_fox(*fox_args, layer=l, pages=pages_fox)
        p_rows.append((jnp.transpose(logf, (1, 2, 0)),
                       state[:, 1:, :w_pool], state[:, HALO_ROWS - state_conv.shape[2]:, w_pool:]))

        xs = _merge(xs, s_yad.reshape(n_seq * t_new, 2 * w_pool), s_yb.reshape(n_seq * t_new, -1),
                    s_yc.reshape(n_seq * t_new, -1), w_gate, w_br, w_out, lng(1), lnb(1), layer=l, tm=tm_s,
                    alpha=alpha)
        xs = _ffn_ln_stream(xs, *ffn[1], lng(2), lnb(2), layer=l, tm=tm_s, tf=tf_s, alpha=alpha)
        s_rows.append((s_kd.reshape(n_seq, t_new, h_diff, -1), s_vd.reshape(n_seq, t_new, h_diff, -1),
                       s_kf.reshape(n_seq, t_new, h_fox, -1), s_vf.reshape(n_seq, t_new, h_fox, -1),
                       jnp.transpose(s_logf, (1, 2, 0)),
                       s_state[:, 1:, :w_pool], s_state[:, HALO_ROWS - state_conv.shape[2]:, w_pool:]))

    p_small = [jnp.stack([r[i] for r in p_rows], axis=0) for i in range(3)]
    s_out = [jnp.stack([r[i] for r in s_rows], axis=0) for i in range(7)]
    kd_all, vd_all, kf_all, vf_all = p_kv
    fox_rows = lambda a: jnp.transpose(a.reshape(depth, batch, h_fox, HEAD_DIM, seq), (0, 1, 4, 2, 3))
    return (xp.reshape(batch, seq, d), xs.reshape(n_seq, t_new, d),
            kd_all.reshape(depth, batch, seq, h_diff, 2 * HEAD_DIM),
            vd_all.reshape(depth, batch, seq, h_diff, 2 * HEAD_DIM),
            fox_rows(kf_all), fox_rows(vf_all), *p_small, *s_out)
```
